```python
import jax
import jax.numpy as jnp
from jax import lax
import numpy as np

D_MODEL = 4096
BATCH = 4
SEQ = 2048
DEPTH = 4
DEC_BATCH = 8
DEC_SEQ = 1
PAST_LEN = 8192
PAGE_SIZE = 128

N_MIXERS = 3
D_FF = 11008
LN_EPS = 1e-5
SB_HEADS = 32
SB_HEAD_DIM = D_MODEL // SB_HEADS
SB_BLOCK = 128
SB_BIAS_INIT = -6.0
RET_HEADS = 16
RET_DK = D_MODEL // RET_HEADS
RET_DV = 2 * RET_DK
RET_CHUNK = 128
ROPE_BASE = 10000.0
ML_HEADS = 8
ML_DK = D_MODEL // (2 * ML_HEADS)
ML_DV = D_MODEL // ML_HEADS
ML_CHUNK = 128
GATE_SOFTCAP = 15.0
N_SB_LAYERS = (DEPTH + 2) // 3
N_RET_LAYERS = (DEPTH + 1) // 3
N_ML_LAYERS = DEPTH // 3
DEEPNORM_ALPHA = (2.0 * DEPTH) ** 0.25
DEEPNORM_BETA = (8.0 * DEPTH) ** -0.25

kernel_name = "hybrid_stickbreak_retention_mlstm_step"


def layer_norm(x, g, b):
    xf = x.astype(jnp.float32)
    mu = jnp.mean(xf, axis=-1, keepdims=True)
    var = jnp.mean(jnp.square(xf - mu), axis=-1, keepdims=True)
    y = (xf - mu) * lax.rsqrt(var + LN_EPS) * g.astype(jnp.float32) + b.astype(jnp.float32)
    return y.astype(x.dtype)


def deepnorm_residual(x, sub, g, b):
    return layer_norm(DEEPNORM_ALPHA * x + sub, g, b)


def half_step_swiglu(x, w_in, w_out):
    gate, up = jnp.split(x @ w_in, 2, axis=-1)
    return 0.5 * ((jax.nn.silu(gate) * up) @ w_out)


def chunked_scan(step, carry, xs, chunk):
    length = xs[0].shape[1]
    c = chunk if length % chunk == 0 else length
    n = length // c

    def to_chunks(a):
        return jnp.swapaxes(a.reshape(a.shape[0], n, c, *a.shape[2:]), 0, 1)

    carry, ys = lax.scan(step, carry, tuple(to_chunks(a) for a in xs))
    ys = jnp.swapaxes(ys, 0, 1)
    return carry, ys.reshape(ys.shape[0], length, *ys.shape[3:])


def stick_breaking_attention(q, k, v, bias, q_offset):
    n_q = q.shape[1]
    scale = SB_HEAD_DIM ** -0.5
    bias = bias.astype(jnp.float32)[None, :, None, None]
    blocks = []
    for start in range(0, n_q, SB_BLOCK):
        stop = min(start + SB_BLOCK, n_q)
        n_keys = q_offset + stop
        kb, vb = k[:, :n_keys], v[:, :n_keys]
        z = jnp.einsum("bqhd,bkhd->bhqk", q[:, start:stop], kb, preferred_element_type=jnp.float32) * scale + bias
        q_pos = q_offset + start + jnp.arange(stop - start)
        k_pos = jnp.arange(n_keys)
        visible = k_pos[None, :] < q_pos[:, None]
        log_beta = jax.nn.log_sigmoid(z)
        log_1m_beta = jnp.where(visible, log_beta - z, 0.0)
        shifted = jnp.concatenate([log_1m_beta[..., 1:], jnp.zeros_like(log_1m_beta[..., :1])], axis=-1)
        log_stick = lax.cumsum(shifted, axis=3, reverse=True)
        weights = jnp.where(visible, jnp.exp(log_beta + log_stick), 0.0)
        blocks.append(jnp.einsum("bhqk,bkhd->bqhd", weights.astype(vb.dtype), vb))
    return jnp.concatenate(blocks, axis=1)


def stick_breaking_mixer(x, w_in, w_out, bias, q_offset, past_k, past_v):
    b, l, _ = x.shape
    q, k, v = jnp.split((x @ w_in).reshape(b, l, 3 * SB_HEADS, SB_HEAD_DIM), 3, axis=2)
    k_all = k if past_k is None else jnp.concatenate([past_k.astype(k.dtype), k], axis=1)
    v_all = v if past_v is None else jnp.concatenate([past_v.astype(v.dtype), v], axis=1)
    o = stick_breaking_attention(q, k_all, v_all, bias, q_offset)
    return o.reshape(b, l, D_MODEL) @ w_out, k, v


def rotary(x, positions):
    half = x.shape[-1] // 2
    inv_freq = ROPE_BASE ** (-jnp.arange(half, dtype=jnp.float32) / half)
    ang = positions.astype(jnp.float32)[:, None] * inv_freq[None, :]
    cos = jnp.cos(ang)[None, :, None, :]
    sin = jnp.sin(ang)[None, :, None, :]
    x1 = x[..., :half].astype(jnp.float32)
    x2 = x[..., half:].astype(jnp.float32)
    return jnp.concatenate([x1 * cos - x2 * sin, x1 * sin + x2 * cos], axis=-1)


def retention_log_decay():
    return jnp.log1p(-jnp.exp2(-5.0 - jnp.arange(RET_HEADS, dtype=jnp.float32)))


def retention_chunk(state, inp):
    q, k, v = inp
    log_g = retention_log_decay()
    c = q.shape[1]
    idx = jnp.arange(c, dtype=jnp.float32)
    diff = idx[:, None] - idx[None, :]
    intra_decay = jnp.where(diff >= 0, jnp.exp(log_g[:, None, None] * jnp.maximum(diff, 0.0)), 0.0)
    scores = jnp.einsum("blhd,bmhd->bhlm", q, k) * intra_decay
    intra = jnp.einsum("bhlm,bmhe->blhe", scores, v)
    q_decay = jnp.exp(log_g[None, :] * (idx[:, None] + 1.0))
    inter = jnp.einsum("blhd,bhde->blhe", q, state) * q_decay[None, :, :, None]
    k_decay = jnp.exp(log_g[None, :] * (c - 1.0 - idx[:, None]))
    new_state = jnp.exp(log_g * c)[None, :, None, None] * state + jnp.einsum("blhd,lh,blhe->bhde", k, k_decay, v)
    return new_state, intra + inter


def retention_mixer(x, state0, w_in, w_out, positions):
    b, l, _ = x.shape
    qk = RET_HEADS * RET_DK
    vd = RET_HEADS * RET_DV
    q, k, v, g = jnp.split(x @ w_in, [qk, 2 * qk, 2 * qk + vd], axis=-1)
    q = rotary(q.reshape(b, l, RET_HEADS, RET_DK), positions)
    k = rotary(k.reshape(b, l, RET_HEADS, RET_DK), positions) * (RET_DK ** -0.5)
    v = v.reshape(b, l, RET_HEADS, RET_DV).astype(jnp.float32)
    state, o = chunked_scan(retention_chunk, state0.astype(jnp.float32), (q, k, v), RET_CHUNK)
    mu = jnp.mean(o, axis=-1, keepdims=True)
    var = jnp.mean(jnp.square(o - mu), axis=-1, keepdims=True)
    o = (o - mu) * lax.rsqrt(var + LN_EPS)
    y = (jax.nn.silu(g) * o.reshape(b, l, vd).astype(x.dtype)) @ w_out
    return y, state


def mlstm_chunk(carry, inp):
    c_mat, n_vec, m_stab = carry
    q, k, v, log_i, log_f = inp
    c = q.shape[1]
    cum_f = jnp.swapaxes(jnp.cumsum(log_f, axis=1), 1, 2)
    log_i_h = jnp.swapaxes(log_i, 1, 2)
    causal = jnp.tril(jnp.ones((c, c), dtype=bool))
    log_w = jnp.where(causal, cum_f[..., :, None] - cum_f[..., None, :] + log_i_h[..., None, :], -jnp.inf)
    log_inter = cum_f + m_stab[..., None]
    m_t = jnp.maximum(log_inter, jnp.max(log_w, axis=-1))
    w = jnp.exp(log_w - m_t[..., None])
    w_inter = jnp.swapaxes(jnp.exp(log_inter - m_t), 1, 2)[..., None]
    s = jnp.einsum("bthd,bshd->bhts", q, k) * w
    num = jnp.einsum("bhts,bshe->bthe", s, v) + jnp.einsum("bthd,bhde->bthe", q, c_mat) * w_inter
    den = jnp.swapaxes(jnp.sum(s, axis=-1), 1, 2) + jnp.einsum("bthd,bhd->bth", q, n_vec) * w_inter[..., 0]
    floor = jnp.swapaxes(jnp.exp(-m_t), 1, 2)
    h = num / jnp.maximum(jnp.abs(den), floor)[..., None]
    m_new = m_t[..., -1]
    decay = jnp.exp(cum_f[..., -1] + m_stab - m_new)
    k_w = jnp.exp(cum_f[..., -1:] - cum_f + log_i_h - m_new[..., None])
    c_new = decay[..., None, None] * c_mat + jnp.einsum("bhs,bshd,bshe->bhde", k_w, k, v)
    n_new = decay[..., None] * n_vec + jnp.einsum("bhs,bshd->bhd", k_w, k)
    return (c_new, n_new, m_new), h


def mlstm_mixer(x, c0, n0, m0, w_in, b_gates, g_norm, w_out):
    b, l, _ = x.shape
    qk = ML_HEADS * ML_DK
    vd = ML_HEADS * ML_DV
    q, k, v, o_gate, gates = jnp.split(x @ w_in, [qk, 2 * qk, 2 * qk + vd, 2 * qk + 2 * vd], axis=-1)
    q = q.reshape(b, l, ML_HEADS, ML_DK).astype(jnp.float32)
    k = k.reshape(b, l, ML_HEADS, ML_DK).astype(jnp.float32) * (ML_DK ** -0.5)
    v = v.reshape(b, l, ML_HEADS, ML_DV).astype(jnp.float32)
    gates = gates.astype(jnp.float32) + b_gates.astype(jnp.float32)
    gates = GATE_SOFTCAP * jnp.tanh(gates / GATE_SOFTCAP)
    log_i, f_pre = jnp.split(gates, 2, axis=-1)
    log_f = jax.nn.log_sigmoid(f_pre)
    carry0 = (c0.astype(jnp.float32), n0.astype(jnp.float32), m0.astype(jnp.float32))
    (c_new, n_new, m_new), h = chunked_scan(mlstm_chunk, carry0, (q, k, v, log_i, log_f), ML_CHUNK)
    h = h * lax.rsqrt(jnp.mean(jnp.square(h), axis=-1, keepdims=True) + LN_EPS)
    h = h * g_norm.astype(jnp.float32).reshape(ML_HEADS, ML_DV)
    y = (jax.nn.sigmoid(o_gate) * h.reshape(b, l, vd).astype(x.dtype)) @ w_out
    return y, c_new, n_new, m_new


def setup_inputs(seed: int = 0) -> dict:
    key = jax.random.key(seed)
    ks = jax.random.split(key, 24)
    f32 = jnp.float32
    n_pages = PAST_LEN // PAGE_SIZE
    n_used = DEC_BATCH * n_pages
    n_phys = n_used + max(1, n_used // 4)
    d = D_MODEL

    def normal(k, shape, scale):
        return jax.random.normal(k, shape, f32) * scale

    x_prompt = normal(ks[0], (BATCH, SEQ, d), 1.0)
    x_sample = normal(ks[1], (DEC_BATCH, DEC_SEQ, d), 1.0)
    cache_k = normal(ks[2], (N_SB_LAYERS, n_phys, PAGE_SIZE, SB_HEADS, SB_HEAD_DIM), 1.0)
    cache_v = normal(ks[3], (N_SB_LAYERS, n_phys, PAGE_SIZE, SB_HEADS, SB_HEAD_DIM), 1.0)
    page_table = jax.random.permutation(ks[4], n_phys)[:n_used].reshape(DEC_BATCH, n_pages).astype(jnp.int32)
    state_ret = normal(ks[5], (N_RET_LAYERS, DEC_BATCH, RET_HEADS, RET_DK, RET_DV), 0.5)
    state_mlstm_c = normal(ks[6], (N_ML_LAYERS, DEC_BATCH, ML_HEADS, ML_DK, ML_DV), 0.5)
    state_mlstm_n = normal(ks[7], (N_ML_LAYERS, DEC_BATCH, ML_HEADS, ML_DK), 0.5)
    state_mlstm_m = normal(ks[8], (N_ML_LAYERS, DEC_BATCH, ML_HEADS), 1.0)
    ln_g = 1.0 + normal(ks[9], (DEPTH, 3, d), 0.02)
    ln_b = normal(ks[10], (DEPTH, 3, d), 0.02)
    w_ffn_in = normal(ks[11], (DEPTH, 2, d, 2 * D_FF), d ** -0.5)
    w_ffn_out = normal(ks[12], (DEPTH, 2, D_FF, d), DEEPNORM_BETA * D_FF ** -0.5)
    w_sb_in = normal(ks[13], (N_SB_LAYERS, d, 3 * d), d ** -0.5)
    w_sb_out = normal(ks[14], (N_SB_LAYERS, d, d), DEEPNORM_BETA * d ** -0.5)
    b_sb = SB_BIAS_INIT + normal(ks[22], (N_SB_LAYERS, SB_HEADS), 0.1)
    ret_cols = 2 * RET_HEADS * RET_DK + 2 * RET_HEADS * RET_DV
    w_ret_in = normal(ks[15], (N_RET_LAYERS, d, ret_cols), d ** -0.5)
    w_ret_out = normal(ks[16], (N_RET_LAYERS, RET_HEADS * RET_DV, d), DEEPNORM_BETA * (RET_HEADS * RET_DV) ** -0.5)
    ml_cols = 2 * ML_HEADS * ML_DK + 2 * ML_HEADS * ML_DV + 2 * ML_HEADS
    w_ml_in = normal(ks[17], (N_ML_LAYERS, d, ml_cols), d ** -0.5)
    b_in = -3.0 + normal(ks[18], (N_ML_LAYERS, ML_HEADS), 0.1)
    b_fg = jnp.linspace(3.0, 6.0, ML_HEADS, dtype=f32)[None, :] + normal(ks[19], (N_ML_LAYERS, ML_HEADS), 0.1)
    b_ml_gates = jnp.concatenate([b_in, b_fg], axis=-1)
    g_ml_norm = 1.0 + normal(ks[20], (N_ML_LAYERS, ML_HEADS * ML_DV), 0.02)
    w_ml_out = normal(ks[21], (N_ML_LAYERS, ML_HEADS * ML_DV, d), DEEPNORM_BETA * (ML_HEADS * ML_DV) ** -0.5)
    return {"x_prompt": x_prompt, "x_sample": x_sample, "cache_k": cache_k, "cache_v": cache_v,
            "page_table": page_table, "state_ret": state_ret, "state_mlstm_c": state_mlstm_c,
            "state_mlstm_n": state_mlstm_n, "state_mlstm_m": state_mlstm_m, "ln_g": ln_g, "ln_b": ln_b,
            "w_ffn_in": w_ffn_in, "w_ffn_out": w_ffn_out, "w_sb_in": w_sb_in, "w_sb_out": w_sb_out,
            "b_sb": b_sb, "w_ret_in": w_ret_in, "w_ret_out": w_ret_out, "w_ml_in": w_ml_in,
            "b_ml_gates": b_ml_gates, "g_ml_norm": g_ml_norm, "w_ml_out": w_ml_out}


def reference(x_prompt, x_sample, cache_k, cache_v, page_table, state_ret, state_mlstm_c, state_mlstm_n,
              state_mlstm_m, ln_g, ln_b, w_ffn_in, w_ffn_out, w_sb_in, w_sb_out, b_sb, w_ret_in, w_ret_out,
              w_ml_in, b_ml_gates, g_ml_norm, w_ml_out):
    n_dec, dec_len, _ = x_sample.shape
    past_len = page_table.shape[1] * cache_k.shape[2]
    pos_prompt = jnp.arange(x_prompt.shape[1], dtype=jnp.int32)
    pos_sample = past_len + jnp.arange(dec_len, dtype=jnp.int32)
    x_p, x_s = x_prompt, x_sample
    k_p, v_p, k_s, v_s = [], [], [], []
    ret_p, ret_s = [], []
    mc_p, mn_p, mm_p, mc_s, mn_s, mm_s = [], [], [], [], [], []
    for i in range(DEPTH):
        j = i // N_MIXERS
        mixer = i % N_MIXERS
        x_p = deepnorm_residual(x_p, half_step_swiglu(x_p, w_ffn_in[i, 0], w_ffn_out[i, 0]), ln_g[i, 0], ln_b[i, 0])
        x_s = deepnorm_residual(x_s, half_step_swiglu(x_s, w_ffn_in[i, 0], w_ffn_out[i, 0]), ln_g[i, 0], ln_b[i, 0])
        if mixer == 0:
            past_k = cache_k[j][page_table].reshape(n_dec, past_len, SB_HEADS, SB_HEAD_DIM)
            past_v = cache_v[j][page_table].reshape(n_dec, past_len, SB_HEADS, SB_HEAD_DIM)
            y_p, kp, vp = stick_breaking_mixer(x_p, w_sb_in[j], w_sb_out[j], b_sb[j], 0, None, None)
            y_s, ks_, vs_ = stick_breaking_mixer(x_s, w_sb_in[j], w_sb_out[j], b_sb[j], past_len, past_k, past_v)
            k_p.append(kp)
            v_p.append(vp)
            k_s.append(ks_)
            v_s.append(vs_)
        elif mixer == 1:
            zero_state = jnp.zeros((x_p.shape[0], RET_HEADS, RET_DK, RET_DV), jnp.float32)
            y_p, sp = retention_mixer(x_p, zero_state, w_ret_in[j], w_ret_out[j], pos_prompt)
            y_s, ss = retention_mixer(x_s, state_ret[j], w_ret_in[j], w_ret_out[j], pos_sample)
            ret_p.append(sp)
            ret_s.append(ss)
        else:
            bp = x_p.shape[0]
            c0 = jnp.zeros((bp, ML_HEADS, ML_DK, ML_DV), jnp.float32)
            n0 = jnp.zeros((bp, ML_HEADS, ML_DK), jnp.float32)
            m0 = jnp.zeros((bp, ML_HEADS), jnp.float32)
            y_p, cp, np_, mp = mlstm_mixer(x_p, c0, n0, m0, w_ml_in[j], b_ml_gates[j], g_ml_norm[j], w_ml_out[j])
            y_s, cs, ns, ms = mlstm_mixer(x_s, state_mlstm_c[j], state_mlstm_n[j], state_mlstm_m[j],
                                          w_ml_in[j], b_ml_gates[j], g_ml_norm[j], w_ml_out[j])
            mc_p.append(cp)
            mn_p.append(np_)
            mm_p.append(mp)
            mc_s.append(cs)
            mn_s.append(ns)
            mm_s.append(ms)
        x_p = deepnorm_residual(x_p, y_p.astype(x_p.dtype), ln_g[i, 1], ln_b[i, 1])
        x_s = deepnorm_residual(x_s, y_s.astype(x_s.dtype), ln_g[i, 1], ln_b[i, 1])
        x_p = deepnorm_residual(x_p, half_step_swiglu(x_p, w_ffn_in[i, 1], w_ffn_out[i, 1]), ln_g[i, 2], ln_b[i, 2])
        x_s = deepnorm_residual(x_s, half_step_swiglu(x_s, w_ffn_in[i, 1], w_ffn_out[i, 1]), ln_g[i, 2], ln_b[i, 2])
    return (x_p, x_s,
            jnp.stack(k_p).astype(cache_k.dtype), jnp.stack(v_p).astype(cache_v.dtype),
            jnp.stack(k_s).astype(cache_k.dtype), jnp.stack(v_s).astype(cache_v.dtype),
            jnp.stack(ret_p).astype(state_ret.dtype), jnp.stack(ret_s).astype(state_ret.dtype),
            jnp.stack(mc_p).astype(state_mlstm_c.dtype), jnp.stack(mn_p).astype(state_mlstm_n.dtype),
            jnp.stack(mm_p).astype(state_mlstm_m.dtype),
            jnp.stack(mc_s).astype(state_mlstm_c.dtype), jnp.stack(mn_s).astype(state_mlstm_n.dtype),
            jnp.stack(mm_s).astype(state_mlstm_m.dtype))
```

```python
import functools

import jax
import jax.numpy as jnp
from jax import lax
from jax.experimental import pallas as pl
from jax.experimental.pallas import tpu as pltpu

F32 = jnp.float32
BF16 = jnp.bfloat16

LN_EPS = 1e-5
N_MIXERS = 3
SB_HEAD_DIM = 128
RET_DK = 256
RET_DV = 512
ML_DK = 256
ML_DV = 512
CHUNK = 128
ROPE_BASE = 10000.0
GATE_SOFTCAP = 15.0
LANES = 128
MIB = 1024 * 1024


def _cparams(semantics, vmem_mib):
    return pltpu.CompilerParams(dimension_semantics=semantics, vmem_limit_bytes=vmem_mib * MIB)


def _tile(dim, pref):
    t = min(dim, pref)
    while dim % t:
        t //= 2
    return t


def _log_sigmoid(z):
    return jnp.minimum(z, 0.0) - jnp.log1p(jnp.exp(-jnp.abs(z)))


def _split_bf16(x):
    hi = x.astype(BF16)
    lo = (x - hi.astype(F32)).astype(BF16)
    return hi, lo


def _mm_kernel(a_ref, w_ref, o_ref, *, nk):
    part = jnp.dot(a_ref[...], w_ref[...], preferred_element_type=F32)
    if nk == 1:
        o_ref[...] = part.astype(o_ref.dtype)
        return
    k = pl.program_id(2)

    @pl.when(k == 0)
    def _():
        o_ref[...] = part

    @pl.when(k != 0)
    def _():
        o_ref[...] += part


def _matmul(a, w, *, tm, tn, tk):
    m, kdim = a.shape
    n = w.shape[1]
    tm, tn, tk = _tile(m, tm), _tile(n, tn), _tile(kdim, tk)
    nk = kdim // tk
    return pl.pallas_call(
        functools.partial(_mm_kernel, nk=nk),
        grid=(m // tm, n // tn, nk),
        in_specs=[pl.BlockSpec((tm, tk), lambda i, j, k: (i, k)),
                  pl.BlockSpec((tk, tn), lambda i, j, k: (k, j))],
        out_specs=pl.BlockSpec((tm, tn), lambda i, j, k: (i, j)),
        out_shape=jax.ShapeDtypeStruct((m, n), F32),
        compiler_params=_cparams(("parallel", "parallel", "arbitrary"), 48),
        name="matmul",
    )(a, w)


def _matmul_wide(a, w):
    return _matmul(a, w, tm=1024, tn=512, tk=a.shape[1])


def _matmul_acc(a, w):
    return _matmul(a, w, tm=1024, tn=2048, tk=512)


def _ffn_in_kernel(x_ref, wg_ref, wu_ref, h_ref):
    x = x_ref[...]
    g = jnp.dot(x, wg_ref[...], preferred_element_type=F32)
    u = jnp.dot(x, wu_ref[...], preferred_element_type=F32)
    h_ref[...] = (g * jax.nn.sigmoid(g) * u).astype(h_ref.dtype)


def _ffn_in(x_b, w_in_b):
    m, d = x_b.shape
    f = w_in_b.shape[1] // 2
    tm, tn = _tile(m, 1024), _tile(f, 256)
    nj = f // tn
    return pl.pallas_call(
        _ffn_in_kernel,
        grid=(m // tm, nj),
        in_specs=[pl.BlockSpec((tm, d), lambda i, j: (i, 0)),
                  pl.BlockSpec((d, tn), lambda i, j: (0, j)),
                  pl.BlockSpec((d, tn), lambda i, j: (0, j + nj))],
        out_specs=pl.BlockSpec((tm, tn), lambda i, j: (i, j)),
        out_shape=jax.ShapeDtypeStruct((m, f), BF16),
        compiler_params=_cparams(("parallel", "arbitrary"), 48),
        name="ffn_in",
    )(x_b, w_in_b, w_in_b)


def _resid_ln_kernel(x_ref, s_ref, g_ref, b_ref, y_ref, yb_ref, *, alpha, scale):
    v = alpha * x_ref[...] + scale * s_ref[...]
    mu = jnp.mean(v, axis=-1, keepdims=True)
    d = v - mu
    var = jnp.mean(d * d, axis=-1, keepdims=True)
    y = d * lax.rsqrt(var + LN_EPS) * g_ref[...] + b_ref[...]
    y_ref[...] = y
    yb_ref[...] = y.astype(BF16)


def _resid_ln(x, sub, g, b, *, alpha, scale):
    m, d = x.shape
    tm = _tile(m, 256)
    row = pl.BlockSpec((tm, d), lambda i: (i, 0))
    vec = pl.BlockSpec((1, d), lambda i: (0, 0))
    return pl.pallas_call(
        functools.partial(_resid_ln_kernel, alpha=alpha, scale=scale),
        grid=(m // tm,),
        in_specs=[row, row, vec, vec],
        out_specs=[row, row],
        out_shape=[jax.ShapeDtypeStruct((m, d), F32), jax.ShapeDtypeStruct((m, d), BF16)],
        compiler_params=_cparams(("parallel",), 48),
        name="resid_ln",
    )(x, sub, g.reshape(1, d), b.reshape(1, d))


def _sb_prompt_kernel(q_ref, k_ref, v_ref, bias_ref, tri_ref, o_ref, *, scale):
    i = pl.program_id(2)
    blk = q_ref.shape[0]
    q = q_ref[...].astype(BF16)
    bias = bias_ref[0]
    tri = tri_ref[...]
    t_idx = lax.broadcasted_iota(jnp.int32, (blk, blk), 0)
    s_idx = lax.broadcasted_iota(jnp.int32, (blk, blk), 1)
    visible = s_idx < t_idx

    def sweep(kb, carry, acc, diagonal):
        off = pl.multiple_of(kb * blk, blk)
        kblk = k_ref[pl.ds(off, blk), :].astype(BF16)
        vblk = v_ref[pl.ds(off, blk), :].astype(BF16)
        z = lax.dot_general(q, kblk, (((1,), (1,)), ((), ())), preferred_element_type=F32) * scale + bias
        log_beta = _log_sigmoid(z)
        l1m = log_beta - z
        if diagonal:
            l1m = jnp.where(visible, l1m, 0.0)
        hi, lo = _split_bf16(l1m)
        stick = (jnp.dot(hi, tri, preferred_element_type=F32)
                 + jnp.dot(lo, tri, preferred_element_type=F32)) + carry
        wgt = jnp.exp(log_beta + stick)
        if diagonal:
            wgt = jnp.where(visible, wgt, 0.0)
        acc = acc + jnp.dot(wgt.astype(BF16), vblk, preferred_element_type=F32)
        carry = carry + jnp.sum(l1m, axis=1, keepdims=True)
        return carry, acc

    carry0 = jnp.zeros((blk, 1), F32)
    acc0 = jnp.zeros((blk, v_ref.shape[1]), F32)
    carry, acc = sweep(i, carry0, acc0, True)

    def body(it, state):
        return sweep(i - 1 - it, state[0], state[1], False)

    carry, acc = lax.fori_loop(0, i, body, (carry, acc))
    o_ref[...] = acc.astype(o_ref.dtype)


def _tri_lower(n):
    j = lax.broadcasted_iota(jnp.int32, (n, n), 0)
    s = lax.broadcasted_iota(jnp.int32, (n, n), 1)
    return (j > s).astype(BF16)


def _sb_prompt(qkv, bias, *, batch, length, heads):
    hd = SB_HEAD_DIM
    blk = CHUNK
    nq = length // blk
    return pl.pallas_call(
        functools.partial(_sb_prompt_kernel, scale=hd ** -0.5),
        grid=(batch, heads, nq),
        in_specs=[pl.BlockSpec((blk, hd), lambda b, h, i: (b * nq + i, h)),
                  pl.BlockSpec((length, hd), lambda b, h, i: (b, heads + h)),
                  pl.BlockSpec((length, hd), lambda b, h, i: (b, 2 * heads + h)),
                  pl.BlockSpec((1, 1, 1), lambda b, h, i: (h, 0, 0)),
                  pl.BlockSpec((blk, blk), lambda b, h, i: (0, 0))],
        out_specs=pl.BlockSpec((blk, hd), lambda b, h, i: (b * nq + i, h)),
        out_shape=jax.ShapeDtypeStruct((batch * length, heads * hd), BF16),
        compiler_params=_cparams(("parallel", "parallel", "arbitrary"), 32),
        name="sb_prompt",
    )(qkv, qkv, qkv, bias.astype(F32).reshape(heads, 1, 1), _tri_lower(blk))


def _sb_sample_kernel(pt_ref, q_ref, kp_ref, vp_ref, bias_ref, e_ref, et_ref, tri_ref, o_ref,
                      carry_sc, acc_sc, *, scale, n_pages):
    del pt_ref
    p = pl.program_id(1)

    @pl.when(p == 0)
    def _():
        carry_sc[...] = jnp.zeros_like(carry_sc)
        acc_sc[...] = jnp.zeros_like(acc_sc)

    kq = kp_ref[...] * q_ref[0]
    hi, lo = _split_bf16(kq)
    e = e_ref[...]
    z = (jnp.dot(hi, e, preferred_element_type=F32) + jnp.dot(lo, e, preferred_element_type=F32)) * scale
    z = z + bias_ref[...]
    log_beta = _log_sigmoid(z)
    l1m = log_beta - z
    hi, lo = _split_bf16(l1m)
    tri = tri_ref[...]
    stick = (jnp.dot(tri, hi, preferred_element_type=F32)
             + jnp.dot(tri, lo, preferred_element_type=F32)) + carry_sc[...]
    wgt = jnp.exp(log_beta + stick)
    carry_sc[...] += jnp.sum(l1m, axis=0, keepdims=True)
    wexp = jnp.dot(wgt.astype(BF16), et_ref[...], preferred_element_type=F32)
    acc_sc[...] += jnp.sum(wexp * vp_ref[...], axis=0, keepdims=True)

    @pl.when(p == n_pages - 1)
    def _():
        o_ref[0] = acc_sc[...]


def _sb_sample(q, cache_k, cache_v, page_table, layer, bias):
    nb, d = q.shape
    heads = d // SB_HEAD_DIM
    n_layers, n_phys, page = cache_k.shape[:3]
    n_pages = page_table.shape[1]
    ck = cache_k.reshape(n_layers, n_phys, page, d)
    cv = cache_v.reshape(n_layers, n_phys, page, d)
    head_of = lax.broadcasted_iota(jnp.int32, (d, heads), 0) // SB_HEAD_DIM
    e = (head_of == lax.broadcasted_iota(jnp.int32, (d, heads), 1)).astype(BF16)
    page_spec = pl.BlockSpec((None, None, page, d),
                             lambda b, p, pt: (layer, pt[b, n_pages - 1 - p], 0, 0))
    const = lambda shape: pl.BlockSpec(shape, lambda b, p, pt: (0,) * len(shape))
    out = pl.pallas_call(
        functools.partial(_sb_sample_kernel, scale=SB_HEAD_DIM ** -0.5, n_pages=n_pages),
        grid_spec=pltpu.PrefetchScalarGridSpec(
            num_scalar_prefetch=1,
            grid=(nb, n_pages),
            in_specs=[pl.BlockSpec((1, 1, d), lambda b, p, pt: (b, 0, 0)),
                      page_spec, page_spec,
                      const((1, heads)), const((d, heads)), const((heads, d)), const((page, page))],
            out_specs=pl.BlockSpec((1, 1, d), lambda b, p, pt: (b, 0, 0)),
            scratch_shapes=[pltpu.VMEM((1, heads), F32), pltpu.VMEM((1, d), F32)]),
        out_shape=jax.ShapeDtypeStruct((nb, 1, d), F32),
        compiler_params=_cparams(("parallel", "arbitrary"), 48),
        name="sb_sample",
    )(page_table, q.reshape(nb, 1, d), ck, cv, bias.astype(F32).reshape(1, heads), e, e.T,
      _tri_lower(page).T)
    return out.reshape(nb, d)


def _rope_tables(positions):
    half = RET_DK // 2
    inv_freq = ROPE_BASE ** (-jnp.arange(half, dtype=F32) / half)
    ang = positions.astype(F32)[:, None] * inv_freq[None, :]
    return jnp.cos(ang), jnp.sin(ang)


def _ret_log_decay(heads):
    return jnp.log1p(-jnp.exp2(-5.0 - jnp.arange(heads, dtype=F32)))


def _group_norm_gate(o, g):
    mu = jnp.mean(o, axis=-1, keepdims=True)
    d = o - mu
    var = jnp.mean(d * d, axis=-1, keepdims=True)
    return g * jax.nn.sigmoid(g) * (d * lax.rsqrt(var + LN_EPS))


def _ret_prompt_kernel(q_ref, k_ref, v_ref, g_ref, cos_ref, sin_ref, lg_ref, o_ref, s_out, s_sc, *, n_chunks):
    c = pl.program_id(2)
    chunk = q_ref.shape[0]
    half = RET_DK // 2

    @pl.when(c == 0)
    def _():
        s_sc[...] = jnp.zeros_like(s_sc)

    lg = lg_ref[0]
    cos = cos_ref[...]
    sin = sin_ref[...]

    def rot(x):
        x1, x2 = x[:, :half], x[:, half:]
        return jnp.concatenate([x1 * cos - x2 * sin, x1 * sin + x2 * cos], axis=1)

    qb = rot(q_ref[...]).astype(BF16)
    kr = rot(k_ref[...]) * (RET_DK ** -0.5)
    vb = v_ref[...].astype(BF16)
    l_idx = lax.broadcasted_iota(jnp.int32, (chunk, chunk), 0)
    m_idx = lax.broadcasted_iota(jnp.int32, (chunk, chunk), 1)
    diff = (l_idx - m_idx).astype(F32)
    decay = jnp.where(diff >= 0, jnp.exp(lg * jnp.maximum(diff, 0.0)), 0.0)
    scores = lax.dot_general(qb, kr.astype(BF16), (((1,), (1,)), ((), ())), preferred_element_type=F32) * decay
    intra = jnp.dot(scores.astype(BF16), vb, preferred_element_type=F32)
    state = s_sc[...]
    pos = lax.broadcasted_iota(jnp.int32, (chunk, 1), 0).astype(F32)
    inter = jnp.dot(qb, state.astype(BF16), preferred_element_type=F32) * jnp.exp(lg * (pos + 1.0))
    kd = kr * jnp.exp(lg * (chunk - 1.0 - pos))
    new_state = jnp.exp(lg * chunk) * state + jnp.dot(kd.T.astype(BF16), vb, preferred_element_type=F32)
    s_sc[...] = new_state
    o_ref[...] = _group_norm_gate(intra + inter, g_ref[...]).astype(o_ref.dtype)

    @pl.when(c == n_chunks - 1)
    def _():
        s_out[0, 0] = new_state


def _ret_prompt(proj, *, batch, length, heads):
    chunk = CHUNK if length % CHUNK == 0 else length
    nc = length // chunk
    cos, sin = _rope_tables(jnp.arange(length, dtype=jnp.int32))
    row = lambda b, h, c: b * nc + c
    return pl.pallas_call(
        functools.partial(_ret_prompt_kernel, n_chunks=nc),
        grid=(batch, heads, nc),
        in_specs=[pl.BlockSpec((chunk, RET_DK), lambda b, h, c: (row(b, h, c), h)),
                  pl.BlockSpec((chunk, RET_DK), lambda b, h, c: (row(b, h, c), heads + h)),
                  pl.BlockSpec((chunk, RET_DV), lambda b, h, c: (row(b, h, c), heads + h)),
                  pl.BlockSpec((chunk, RET_DV), lambda b, h, c: (row(b, h, c), 2 * heads + h)),
                  pl.BlockSpec((chunk, RET_DK // 2), lambda b, h, c: (c, 0)),
                  pl.BlockSpec((chunk, RET_DK // 2), lambda b, h, c: (c, 0)),
                  pl.BlockSpec((1, 1, 1), lambda b, h, c: (h, 0, 0))],
        out_specs=[pl.BlockSpec((chunk, RET_DV), lambda b, h, c: (row(b, h, c), h)),
                   pl.BlockSpec((1, 1, RET_DK, RET_DV), lambda b, h, c: (b, h, 0, 0))],
        out_shape=[jax.ShapeDtypeStruct((batch * length, heads * RET_DV), BF16),
                   jax.ShapeDtypeStruct((batch, heads, RET_DK, RET_DV), F32)],
        scratch_shapes=[pltpu.VMEM((RET_DK, RET_DV), F32)],
        compiler_params=_cparams(("parallel", "parallel", "arbitrary"), 32),
        name="ret_prompt",
    )(proj, proj, proj, proj, cos, sin, _ret_log_decay(heads).reshape(heads, 1, 1))


def _ret_sample_kernel(q_ref, k_ref, v_ref, g_ref, cos_ref, sin_ref, lg_ref, s_ref, o_ref, s_out):
    half = RET_DK // 2
    lg = lg_ref[0]
    cos = cos_ref[...]
    sin = sin_ref[...]

    def rot(x):
        x1, x2 = x[:half], x[half:]
        return jnp.concatenate([x1 * cos - x2 * sin, x1 * sin + x2 * cos], axis=0)

    q = rot(q_ref[0, 0])
    k = rot(k_ref[0, 0]) * (RET_DK ** -0.5)
    v = v_ref[0, 0]
    state = s_ref[0, 0, 0]
    gamma = jnp.exp(lg)
    intra = jnp.sum(q * k, axis=0, keepdims=True) * v
    inter = jnp.sum(q * state, axis=0, keepdims=True) * gamma
    s_out[0, 0] = gamma * state + k * v
    o_ref[0, 0] = _group_norm_gate(intra + inter, g_ref[0, 0])


def _ret_sample(proj, state_ret, layer, *, position):
    nb = proj.shape[0]
    heads = state_ret.shape[2]
    qk = heads * RET_DK
    vd = heads * RET_DV
    q = proj[:, :qk].reshape(nb, heads, RET_DK, 1)
    k = proj[:, qk:2 * qk].reshape(nb, heads, RET_DK, 1)
    v = proj[:, 2 * qk:2 * qk + vd].reshape(nb, heads, 1, RET_DV)
    g = proj[:, 2 * qk + vd:].reshape(nb, heads, 1, RET_DV)
    cos, sin = _rope_tables(jnp.full((1,), position, jnp.int32))
    half = RET_DK // 2
    col = pl.BlockSpec((1, 1, RET_DK, 1), lambda b, h: (b, h, 0, 0))
    rowv = pl.BlockSpec((1, 1, 1, RET_DV), lambda b, h: (b, h, 0, 0))
    tab = pl.BlockSpec((half, 1), lambda b, h: (0, 0))
    o, s_new = pl.pallas_call(
        _ret_sample_kernel,
        grid=(nb, heads),
        in_specs=[col, col, rowv, rowv, tab, tab,
                  pl.BlockSpec((1, 1, 1), lambda b, h: (h, 0, 0)),
                  pl.BlockSpec((1, 1, 1, RET_DK, RET_DV), lambda b, h: (layer, b, h, 0, 0))],
        out_specs=[rowv, pl.BlockSpec((1, 1, RET_DK, RET_DV), lambda b, h: (b, h, 0, 0))],
        out_shape=[jax.ShapeDtypeStruct((nb, heads, 1, RET_DV), F32),
                   jax.ShapeDtypeStruct((nb, heads, RET_DK, RET_DV), F32)],
        compiler_params=_cparams(("parallel", "parallel"), 32),
        name="ret_sample",
    )(q, k, v, g, cos.reshape(half, 1), sin.reshape(half, 1),
      _ret_log_decay(heads).reshape(heads, 1, 1), state_ret)
    return o.reshape(nb, vd), s_new


def _softcap(x):
    return GATE_SOFTCAP * jnp.tanh(x / GATE_SOFTCAP)


def _rms_gate(h, gnorm, og):
    hn = h * lax.rsqrt(jnp.mean(h * h, axis=-1, keepdims=True) + LN_EPS) * gnorm
    return jax.nn.sigmoid(og) * hn


def _ml_prompt_kernel(q_ref, k_ref, v_ref, og_ref, gt_ref, gb_ref, gn_ref, h_ref, c_out, n_out, m_out,
                      c_sc, n_sc, m_sc, *, n_chunks):
    c = pl.program_id(2)
    chunk = q_ref.shape[0]

    @pl.when(c == 0)
    def _():
        c_sc[...] = jnp.zeros_like(c_sc)
        n_sc[...] = jnp.zeros_like(n_sc)
        m_sc[...] = jnp.zeros_like(m_sc)

    gates = _softcap(gt_ref[0, 0] + gb_ref[0])
    li_row = gates[0:1, :]
    lf_row = _log_sigmoid(gates[1:2, :])
    t_idx = lax.broadcasted_iota(jnp.int32, (chunk, chunk), 0)
    s_idx = lax.broadcasted_iota(jnp.int32, (chunk, chunk), 1)
    eye = t_idx == s_idx
    causal = s_idx <= t_idx
    lf_col = jnp.sum(jnp.where(eye, lf_row, 0.0), axis=1, keepdims=True)
    li_col = jnp.sum(jnp.where(eye, li_row, 0.0), axis=1, keepdims=True)
    cum_col = jnp.sum(jnp.where(causal, lf_row, 0.0), axis=1, keepdims=True)
    cum_row = jnp.sum(jnp.where(t_idx <= s_idx, lf_col, 0.0), axis=0, keepdims=True)
    m_prev = m_sc[...]
    log_w = jnp.where(causal, cum_col - cum_row + li_row, -jnp.inf)
    log_inter = cum_col + m_prev
    m_t = jnp.maximum(log_inter, jnp.max(log_w, axis=1, keepdims=True))
    w = jnp.exp(log_w - m_t)
    w_inter = jnp.exp(log_inter - m_t)
    q = q_ref[...]
    qb = q.astype(BF16)
    k = k_ref[...] * (ML_DK ** -0.5)
    vb = v_ref[...].astype(BF16)
    s = lax.dot_general(qb, k.astype(BF16), (((1,), (1,)), ((), ())), preferred_element_type=F32) * w
    c_mat = c_sc[...]
    n_vec = n_sc[...]
    num = (jnp.dot(s.astype(BF16), vb, preferred_element_type=F32)
           + jnp.dot(qb, c_mat.astype(BF16), preferred_element_type=F32) * w_inter)
    den = jnp.sum(s, axis=1, keepdims=True) + jnp.sum(q * n_vec, axis=1, keepdims=True) * w_inter
    h = num / jnp.maximum(jnp.abs(den), jnp.exp(-m_t))
    cum_last = cum_col[chunk - 1:chunk, :]
    m_new = m_t[chunk - 1:chunk, :]
    decay = jnp.exp(cum_last + m_prev - m_new)
    kk = k * jnp.exp(cum_last - cum_col + li_col - m_new)
    c_new = decay * c_mat + jnp.dot(kk.T.astype(BF16), vb, preferred_element_type=F32)
    n_new = decay * n_vec + jnp.sum(kk, axis=0, keepdims=True)
    c_sc[...] = c_new
    n_sc[...] = n_new
    m_sc[...] = m_new
    h_ref[...] = _rms_gate(h, gn_ref[0], og_ref[...]).astype(h_ref.dtype)

    @pl.when(c == n_chunks - 1)
    def _():
        c_out[0, 0] = c_new
        n_out[0, 0] = n_new
        m_out[0, 0] = m_new


def _ml_prompt(proj, gates, b_gates, g_norm, *, batch, length, heads):
    chunk = CHUNK if length % CHUNK == 0 else length
    nc = length // chunk
    gt = gates.reshape(batch, length, 2, heads).transpose(0, 3, 2, 1)
    gb = b_gates.astype(F32).reshape(2, heads).T.reshape(heads, 2, 1)
    row = lambda b, h, c: b * nc + c
    hg, c_new, n_new, m_new = pl.pallas_call(
        functools.partial(_ml_prompt_kernel, n_chunks=nc),
        grid=(batch, heads, nc),
        in_specs=[pl.BlockSpec((chunk, ML_DK), lambda b, h, c: (row(b, h, c), h)),
                  pl.BlockSpec((chunk, ML_DK), lambda b, h, c: (row(b, h, c), heads + h)),
                  pl.BlockSpec((chunk, ML_DV), lambda b, h, c: (row(b, h, c), heads + h)),
                  pl.BlockSpec((chunk, ML_DV), lambda b, h, c: (row(b, h, c), 2 * heads + h)),
                  pl.BlockSpec((1, 1, 2, chunk), lambda b, h, c: (b, h, 0, c)),
                  pl.BlockSpec((1, 2, 1), lambda b, h, c: (h, 0, 0)),
                  pl.BlockSpec((1, 1, ML_DV), lambda b, h, c: (h, 0, 0))],
        out_specs=[pl.BlockSpec((chunk, ML_DV), lambda b, h, c: (row(b, h, c), h)),
                   pl.BlockSpec((1, 1, ML_DK, ML_DV), lambda b, h, c: (b, h, 0, 0)),
                   pl.BlockSpec((1, 1, 1, ML_DK), lambda b, h, c: (b, h, 0, 0)),
                   pl.BlockSpec((1, 1, 1, 1), lambda b, h, c: (b, h, 0, 0))],
        out_shape=[jax.ShapeDtypeStruct((batch * length, heads * ML_DV), BF16),
                   jax.ShapeDtypeStruct((batch, heads, ML_DK, ML_DV), F32),
                   jax.ShapeDtypeStruct((batch, heads, 1, ML_DK), F32),
                   jax.ShapeDtypeStruct((batch, heads, 1, 1), F32)],
        scratch_shapes=[pltpu.VMEM((ML_DK, ML_DV), F32), pltpu.VMEM((1, ML_DK), F32), pltpu.VMEM((1, 1), F32)],
        compiler_params=_cparams(("parallel", "parallel", "arbitrary"), 32),
        name="ml_prompt",
    )(proj, proj, proj, proj, gt, gb, g_norm.astype(F32).reshape(heads, 1, ML_DV))
    return hg, c_new, n_new.reshape(batch, heads, ML_DK), m_new.reshape(batch, heads)


def _ml_sample_kernel(q_ref, k_ref, v_ref, og_ref, gt_ref, gb_ref, gn_ref, c_ref, n_ref, m_ref,
                      h_ref, c_out, n_out, m_out):
    q = q_ref[0, 0]
    k = k_ref[0, 0] * (ML_DK ** -0.5)
    v = v_ref[0, 0]
    gates = _softcap(gt_ref[0, 0] + gb_ref[0])
    log_i = gates[0:1, :]
    log_f = _log_sigmoid(gates[1:2, :])
    c_mat = c_ref[0, 0, 0]
    n_vec = n_ref[0, 0, 0]
    m_prev = m_ref[0, 0, 0]
    log_inter = log_f + m_prev
    m_t = jnp.maximum(log_inter, log_i)
    w = jnp.exp(log_i - m_t)
    w_inter = jnp.exp(log_inter - m_t)
    s = jnp.sum(q * k, axis=0, keepdims=True) * w
    num = s * v + jnp.sum(q * c_mat, axis=0, keepdims=True) * w_inter
    den = s + jnp.sum(q * n_vec, axis=0, keepdims=True) * w_inter
    h = num / jnp.maximum(jnp.abs(den), jnp.exp(-m_t))
    decay = jnp.exp(log_f + m_prev - m_t)
    kk = k * jnp.exp(log_i - m_t)
    c_out[0, 0] = decay * c_mat + kk * v
    n_out[0, 0] = decay * n_vec + kk
    m_out[0, 0] = m_t
    h_ref[0, 0] = _rms_gate(h, gn_ref[0], og_ref[0, 0])


def _ml_sample(proj, gates, b_gates, g_norm, state_c, state_n, state_m, layer):
    nb = proj.shape[0]
    heads = state_c.shape[2]
    qk = heads * ML_DK
    vd = heads * ML_DV
    q = proj[:, :qk].reshape(nb, heads, ML_DK, 1)
    k = proj[:, qk:2 * qk].reshape(nb, heads, ML_DK, 1)
    v = proj[:, 2 * qk:2 * qk + vd].reshape(nb, heads, 1, ML_DV)
    og = proj[:, 2 * qk + vd:].reshape(nb, heads, 1, ML_DV)
    gt = gates.reshape(nb, 2, heads).transpose(0, 2, 1).reshape(nb, heads, 2, 1)
    gb = b_gates.astype(F32).reshape(2, heads).T.reshape(heads, 2, 1)
    n_layers = state_c.shape[0]
    col = pl.BlockSpec((1, 1, ML_DK, 1), lambda b, h: (b, h, 0, 0))
    rowv = pl.BlockSpec((1, 1, 1, ML_DV), lambda b, h: (b, h, 0, 0))
    mat = pl.BlockSpec((1, 1, ML_DK, ML_DV), lambda b, h: (b, h, 0, 0))
    one = pl.BlockSpec((1, 1, 1, 1), lambda b, h: (b, h, 0, 0))
    hg, c_new, n_new, m_new = pl.pallas_call(
        _ml_sample_kernel,
        grid=(nb, heads),
        in_specs=[col, col, rowv, rowv,
                  pl.BlockSpec((1, 1, 2, 1), lambda b, h: (b, h, 0, 0)),
                  pl.BlockSpec((1, 2, 1), lambda b, h: (h, 0, 0)),
                  pl.BlockSpec((1, 1, ML_DV), lambda b, h: (h, 0, 0)),
                  pl.BlockSpec((1, 1, 1, ML_DK, ML_DV), lambda b, h: (layer, b, h, 0, 0)),
                  pl.BlockSpec((1, 1, 1, ML_DK, 1), lambda b, h: (layer, b, h, 0, 0)),
                  pl.BlockSpec((1, 1, 1, 1, 1), lambda b, h: (layer, b, h, 0, 0))],
        out_specs=[rowv, mat, col, one],
        out_shape=[jax.ShapeDtypeStruct((nb, heads, 1, ML_DV), F32),
                   jax.ShapeDtypeStruct((nb, heads, ML_DK, ML_DV), F32),
                   jax.ShapeDtypeStruct((nb, heads, ML_DK, 1), F32),
                   jax.ShapeDtypeStruct((nb, heads, 1, 1), F32)],
        compiler_params=_cparams(("parallel", "parallel"), 32),
        name="ml_sample",
    )(q, k, v, og, gt, gb, g_norm.astype(F32).reshape(heads, 1, ML_DV), state_c,
      state_n.reshape(n_layers, nb, heads, ML_DK, 1), state_m.reshape(n_layers, nb, heads, 1, 1))
    return hg.reshape(nb, vd), c_new, n_new.reshape(nb, heads, ML_DK), m_new.reshape(nb, heads)


def kernel(x_prompt, x_sample, cache_k, cache_v, page_table, state_ret, state_mlstm_c, state_mlstm_n,
           state_mlstm_m, ln_g, ln_b, w_ffn_in, w_ffn_out, w_sb_in, w_sb_out, b_sb, w_ret_in, w_ret_out,
           w_ml_in, b_ml_gates, g_ml_norm, w_ml_out):
    batch, length, d = x_prompt.shape
    nb, dec_len, _ = x_sample.shape
    assert dec_len == 1, "the sample group is decoded one token per sequence"
    depth = ln_g.shape[0]
    alpha = (2.0 * depth) ** 0.25
    past_len = page_table.shape[1] * cache_k.shape[2]
    sb_heads = d // SB_HEAD_DIM
    ret_heads = d // RET_DK
    ml_heads = d // ML_DV

    groups = [x_prompt.reshape(batch * length, d), x_sample.reshape(nb, d)]
    groups = [(x, x.astype(BF16)) for x in groups]

    def ffn_half(groups, i, half, ln_idx):
        w_in = w_ffn_in[i, half].astype(BF16)
        w_out = w_ffn_out[i, half].astype(BF16)
        out = []
        for x, xb in groups:
            sub = _matmul_acc(_ffn_in(xb, w_in), w_out)
            out.append(_resid_ln(x, sub, ln_g[i, ln_idx], ln_b[i, ln_idx], alpha=alpha, scale=0.5))
        return out

    k_p, v_p, k_s, v_s, ret_p, ret_s = [], [], [], [], [], []
    mc_p, mn_p, mm_p, mc_s, mn_s, mm_s = [], [], [], [], [], []
    for i in range(depth):
        j, mixer = divmod(i, N_MIXERS)
        groups = ffn_half(groups, i, 0, 0)
        (xp, xpb), (xs, xsb) = groups
        if mixer == 0:
            w_in = w_sb_in[j].astype(BF16)
            w_out = w_sb_out[j].astype(BF16)
            qkv_p = _matmul_wide(xpb, w_in)
            qkv_s = _matmul_wide(xsb, w_in)
            o_p = _sb_prompt(qkv_p, b_sb[j], batch=batch, length=length, heads=sb_heads)
            o_s = _sb_sample(qkv_s[:, :d], cache_k, cache_v, page_table, j, b_sb[j]).astype(BF16)
            kv_shape = (sb_heads, SB_HEAD_DIM)
            k_p.append(qkv_p[:, d:2 * d].reshape(batch, length, *kv_shape))
            v_p.append(qkv_p[:, 2 * d:].reshape(batch, length, *kv_shape))
            k_s.append(qkv_s[:, d:2 * d].reshape(nb, 1, *kv_shape))
            v_s.append(qkv_s[:, 2 * d:].reshape(nb, 1, *kv_shape))
        elif mixer == 1:
            w_in = w_ret_in[j].astype(BF16)
            w_out = w_ret_out[j].astype(BF16)
            o_p, sp = _ret_prompt(_matmul_wide(xpb, w_in), batch=batch, length=length, heads=ret_heads)
            o_s, ss = _ret_sample(_matmul_wide(xsb, w_in), state_ret, j, position=past_len)
            o_s = o_s.astype(BF16)
            ret_p.append(sp)
            ret_s.append(ss)
        else:
            n_main = 2 * ml_heads * ML_DK + 2 * ml_heads * ML_DV
            w_in = w_ml_in[j, :, :n_main].astype(BF16)
            w_gate = jnp.pad(w_ml_in[j, :, n_main:], ((0, 0), (0, LANES - 2 * ml_heads))).astype(BF16)
            w_out = w_ml_out[j].astype(BF16)
            gates_p = _matmul(xpb, w_gate, tm=1024, tn=LANES, tk=d)[:, :2 * ml_heads]
            gates_s = _matmul(xsb, w_gate, tm=1024, tn=LANES, tk=d)[:, :2 * ml_heads]
            o_p, cp, np_, mp = _ml_prompt(_matmul_wide(xpb, w_in), gates_p, b_ml_gates[j], g_ml_norm[j],
                                          batch=batch, length=length, heads=ml_heads)
            o_s, cs, ns, ms = _ml_sample(_matmul_wide(xsb, w_in), gates_s, b_ml_gates[j], g_ml_norm[j],
                                         state_mlstm_c, state_mlstm_n, state_mlstm_m, j)
            o_s = o_s.astype(BF16)
            mc_p.append(cp)
            mn_p.append(np_)
            mm_p.append(mp)
            mc_s.append(cs)
            mn_s.append(ns)
            mm_s.append(ms)
        y_p = _matmul_acc(o_p, w_out)
        y_s = _matmul_acc(o_s, w_out)
        groups = [_resid_ln(xp, y_p, ln_g[i, 1], ln_b[i, 1], alpha=alpha, scale=1.0),
                  _resid_ln(xs, y_s, ln_g[i, 1], ln_b[i, 1], alpha=alpha, scale=1.0)]
        groups = ffn_half(groups, i, 1, 2)

    (xp, _), (xs, _) = groups
    return (xp.reshape(batch, length, d), xs.reshape(nb, 1, d),
            jnp.stack(k_p), jnp.stack(v_p), jnp.stack(k_s), jnp.stack(v_s),
            jnp.stack(ret_p), jnp.stack(ret_s),
            jnp.stack(mc_p), jnp.stack(mn_p), jnp.stack(mm_p),
            jnp.stack(mc_s), jnp.stack(mn_s), jnp.stack(mm_s))
```

```python
import functools

import jax
import jax.numpy as jnp
from jax import lax
from jax.experimental import pallas as pl
from jax.experimental.pallas import tpu as pltpu

F32 = jnp.float32
BF16 = jnp.bfloat16

LN_EPS = 1e-5
N_MIXERS = 3
SB_HEAD_DIM = 128
RET_DK = 256
RET_DV = 512
ML_DK = 256
ML_DV = 512
CHUNK = 128
SB_QUERY_TILE = 1024
SB_KEY_TILE = 256
ROPE_BASE = 10000.0
GATE_SOFTCAP = 15.0
LANES = 128
MIB = 1024 * 1024
LOG2E = 1.4426950408889634
MATMUL_PANEL_BUDGET = 36 * MIB
MATMUL_VMEM_BUDGET = 46 * MIB


def _cparams(semantics, vmem_mib):
    return pltpu.CompilerParams(dimension_semantics=semantics, vmem_limit_bytes=vmem_mib * MIB)


def _tile(dim, pref):
    t = min(dim, pref)
    while dim % t:
        t //= 2
    return t


def _log_sigmoid(z):
    return jnp.minimum(z, 0.0) - jnp.log1p(jnp.exp(-jnp.abs(z)))


def _split_bf16(x):
    hi = x.astype(BF16)
    lo = (x - hi.astype(F32)).astype(BF16)
    return hi, lo


def _mm_kernel(a_ref, w_ref, o_ref):
    o_ref[...] = jnp.dot(a_ref[...], w_ref[...], preferred_element_type=F32).astype(o_ref.dtype)


def _matmul(a, w, *, tm=1024, tn=512):
    m, kdim = a.shape
    n = w.shape[1]
    tm, tn = _tile(m, tm), _tile(n, tn)
    panel = tm * kdim * a.dtype.itemsize
    single = 2 * panel > MATMUL_PANEL_BUDGET
    panel *= 1 if single else 2

    def vmem_bytes(tn):
        return panel + 2 * kdim * tn * w.dtype.itemsize + 3 * tm * tn * 4

    while vmem_bytes(tn) > MATMUL_VMEM_BUDGET and tn > LANES:
        tn //= 2
    a_spec = pl.BlockSpec((tm, kdim), lambda i, j: (i, 0),
                          **({"pipeline_mode": pl.Buffered(1)} if single else {}))
    return pl.pallas_call(
        _mm_kernel,
        grid=(m // tm, n // tn),
        in_specs=[a_spec, pl.BlockSpec((kdim, tn), lambda i, j: (0, j))],
        out_specs=pl.BlockSpec((tm, tn), lambda i, j: (i, j)),
        out_shape=jax.ShapeDtypeStruct((m, n), F32),
        compiler_params=_cparams(("parallel", "arbitrary"), vmem_bytes(tn) // MIB + 6),
        name="matmul",
    )(a, w)


def _ffn_in_kernel(x_ref, wg_ref, wu_ref, h_ref):
    x = x_ref[...]
    g = jnp.dot(x, wg_ref[...], preferred_element_type=F32)
    u = jnp.dot(x, wu_ref[...], preferred_element_type=F32)
    h_ref[...] = (g * jax.nn.sigmoid(g) * u).astype(h_ref.dtype)


def _ffn_in(x_b, w_in_b):
    m, d = x_b.shape
    f = w_in_b.shape[1] // 2
    tm, tn = _tile(m, 1024), _tile(f, 256)
    nj = f // tn
    return pl.pallas_call(
        _ffn_in_kernel,
        grid=(m // tm, nj),
        in_specs=[pl.BlockSpec((tm, d), lambda i, j: (i, 0)),
                  pl.BlockSpec((d, tn), lambda i, j: (0, j)),
                  pl.BlockSpec((d, tn), lambda i, j: (0, j + nj))],
        out_specs=pl.BlockSpec((tm, tn), lambda i, j: (i, j)),
        out_shape=jax.ShapeDtypeStruct((m, f), BF16),
        compiler_params=_cparams(("parallel", "arbitrary"), 48),
        name="ffn_in",
    )(x_b, w_in_b, w_in_b)


def _resid_ln_kernel(x_ref, s_ref, g_ref, b_ref, y_ref, yb_ref, *, alpha, scale):
    v = alpha * x_ref[...] + scale * s_ref[...]
    mu = jnp.mean(v, axis=-1, keepdims=True)
    d = v - mu
    var = jnp.mean(d * d, axis=-1, keepdims=True)
    y = d * lax.rsqrt(var + LN_EPS) * g_ref[...] + b_ref[...]
    y_ref[...] = y
    yb_ref[...] = y.astype(BF16)


def _resid_ln(x, sub, g, b, *, alpha, scale):
    m, d = x.shape
    tm = _tile(m, 256)
    row = pl.BlockSpec((tm, d), lambda i: (i, 0))
    vec = pl.BlockSpec((1, d), lambda i: (0, 0))
    return pl.pallas_call(
        functools.partial(_resid_ln_kernel, alpha=alpha, scale=scale),
        grid=(m // tm,),
        in_specs=[row, row, vec, vec],
        out_specs=[row, row],
        out_shape=[jax.ShapeDtypeStruct((m, d), F32), jax.ShapeDtypeStruct((m, d), BF16)],
        compiler_params=_cparams(("parallel",), 48),
        name="resid_ln",
    )(x, sub, g.reshape(1, d), b.reshape(1, d))


def _sb_prompt_kernel(q_ref, k_ref, v_ref, bias_ref, tri_ref, o_ref, *, scale):
    i = pl.program_id(2)
    tq = q_ref.shape[0]
    kw = tri_ref.shape[0]
    n_diag = tq // kw
    hd = v_ref.shape[1]
    q = q_ref[...].astype(BF16)
    bias2 = bias_ref[0] * LOG2E
    scale2 = scale * LOG2E
    tri = tri_ref[...]
    tri2 = jnp.concatenate([tri, tri], axis=0)

    def sweep(kb, carry, acc, first_row, diagonal):
        rows = tq - first_row
        off = pl.multiple_of(kb * kw, kw)
        kblk = k_ref[pl.ds(off, kw), :].astype(BF16)
        vblk = v_ref[pl.ds(off, kw), :].astype(BF16)
        z = lax.dot_general(q[first_row:], kblk, (((1,), (1,)), ((), ())), preferred_element_type=F32)
        z = z * scale2 + bias2
        l1m = -(jnp.maximum(z, 0.0) + jnp.log(1.0 + jnp.exp2(-jnp.abs(z))) * LOG2E)
        log_beta = z + l1m
        if diagonal:
            visible = (lax.broadcasted_iota(jnp.int32, (rows, kw), 1)
                       < lax.broadcasted_iota(jnp.int32, (rows, kw), 0))
            l1m = jnp.where(visible, l1m, 0.0)
        hi, lo = _split_bf16(l1m)
        stick = jnp.dot(jnp.concatenate([hi, lo], axis=1), tri2, preferred_element_type=F32) + carry[first_row:]
        wgt = jnp.exp2(log_beta + stick)
        if diagonal:
            wgt = jnp.where(visible, wgt, 0.0)
        acc_new = acc[first_row:] + jnp.dot(wgt.astype(BF16), vblk, preferred_element_type=F32)
        carry_new = carry[first_row:] + jnp.sum(l1m, axis=1, keepdims=True)
        if first_row:
            acc_new = jnp.concatenate([acc[:first_row], acc_new], axis=0)
            carry_new = jnp.concatenate([carry[:first_row], carry_new], axis=0)
        return carry_new, acc_new

    carry = jnp.zeros((tq, 1), F32)
    acc = jnp.zeros((tq, hd), F32)
    for r in reversed(range(n_diag)):
        carry, acc = sweep(i * n_diag + r, carry, acc, r * kw, True)

    def body(it, state):
        for r in range(n_diag):
            state = sweep((i - it) * n_diag - 1 - r, state[0], state[1], 0, False)
        return state

    carry, acc = lax.fori_loop(0, i, body, (carry, acc))
    o_ref[...] = acc.astype(o_ref.dtype)


def _tri_lower(n):
    j = lax.broadcasted_iota(jnp.int32, (n, n), 0)
    s = lax.broadcasted_iota(jnp.int32, (n, n), 1)
    return (j > s).astype(BF16)


def _sb_prompt(qkv, bias, *, batch, length, heads):
    hd = SB_HEAD_DIM
    tq, kw = _tile(length, SB_QUERY_TILE), _tile(length, SB_KEY_TILE)
    nq = length // tq
    return pl.pallas_call(
        functools.partial(_sb_prompt_kernel, scale=hd ** -0.5),
        grid=(batch, heads, nq),
        in_specs=[pl.BlockSpec((tq, hd), lambda b, h, i: (b * nq + i, h)),
                  pl.BlockSpec((length, hd), lambda b, h, i: (b, heads + h)),
                  pl.BlockSpec((length, hd), lambda b, h, i: (b, 2 * heads + h)),
                  pl.BlockSpec((1, 1, 1), lambda b, h, i: (h, 0, 0)),
                  pl.BlockSpec((kw, kw), lambda b, h, i: (0, 0))],
        out_specs=pl.BlockSpec((tq, hd), lambda b, h, i: (b * nq + i, h)),
        out_shape=jax.ShapeDtypeStruct((batch * length, heads * hd), BF16),
        compiler_params=_cparams(("parallel", "parallel", "arbitrary"), 32),
        name="sb_prompt",
    )(qkv, qkv, qkv, bias.astype(F32).reshape(heads, 1, 1), _tri_lower(kw))


def _sb_sample_kernel(pt_ref, q_ref, kp_ref, vp_ref, bias_ref, o_ref, run_sc, acc_sc, *, scale, n_pages):
    del pt_ref
    p = pl.program_id(1)
    page, heads, hd = kp_ref.shape

    @pl.when(p == 0)
    def _():
        run_sc[...] = jnp.zeros_like(run_sc)
        acc_sc[...] = jnp.zeros_like(acc_sc)

    kq = (kp_ref[...] * q_ref[0]).reshape(page * heads, hd)
    hi, lo = _split_bf16(kq)
    ones = jnp.ones((hd, hd), BF16)
    z = jnp.dot(hi, ones, preferred_element_type=F32) + jnp.dot(lo, ones, preferred_element_type=F32)
    z = z.reshape(page, heads, hd) * (scale * LOG2E) + bias_ref[...] * LOG2E
    l1m = -(jnp.maximum(z, 0.0) + jnp.log(1.0 + jnp.exp2(-jnp.abs(z))) * LOG2E)
    log_beta = z + l1m
    run = run_sc[...]
    acc = acc_sc[...]
    for s in reversed(range(page)):
        acc = acc + jnp.exp2(log_beta[s] + run) * vp_ref[s]
        run = run + l1m[s]
    run_sc[...] = run
    acc_sc[...] = acc

    @pl.when(p == n_pages - 1)
    def _():
        o_ref[0] = acc


def _sb_sample(q, cache_k, cache_v, page_table, layer, bias):
    nb, d = q.shape
    page, heads, hd = cache_k.shape[2:]
    n_pages = page_table.shape[1]
    page_spec = pl.BlockSpec((None, None, page, heads, hd),
                             lambda b, p, pt: (layer, pt[b, n_pages - 1 - p], 0, 0, 0))
    row_spec = pl.BlockSpec((1, heads, hd), lambda b, p, pt: (b, 0, 0))
    out = pl.pallas_call(
        functools.partial(_sb_sample_kernel, scale=hd ** -0.5, n_pages=n_pages),
        grid_spec=pltpu.PrefetchScalarGridSpec(
            num_scalar_prefetch=1,
            grid=(nb, n_pages),
            in_specs=[row_spec, page_spec, page_spec,
                      pl.BlockSpec((heads, 1), lambda b, p, pt: (0, 0))],
            out_specs=row_spec,
            scratch_shapes=[pltpu.VMEM((heads, hd), F32), pltpu.VMEM((heads, hd), F32)]),
        out_shape=jax.ShapeDtypeStruct((nb, heads, hd), F32),
        compiler_params=_cparams(("parallel", "arbitrary"), 32),
        name="sb_sample",
    )(page_table, q.reshape(nb, heads, hd), cache_k, cache_v, bias.astype(F32).reshape(heads, 1))
    return out.reshape(nb, d)


def _rope_tables(positions):
    half = RET_DK // 2
    inv_freq = ROPE_BASE ** (-jnp.arange(half, dtype=F32) / half)
    ang = positions.astype(F32)[:, None] * inv_freq[None, :]
    return jnp.cos(ang), jnp.sin(ang)


def _ret_log_decay(heads):
    return jnp.log1p(-jnp.exp2(-5.0 - jnp.arange(heads, dtype=F32)))


def _group_norm_gate(o, g):
    mu = jnp.mean(o, axis=-1, keepdims=True)
    d = o - mu
    var = jnp.mean(d * d, axis=-1, keepdims=True)
    return g * jax.nn.sigmoid(g) * (d * lax.rsqrt(var + LN_EPS))


def _ret_prompt_kernel(q_ref, k_ref, v_ref, g_ref, cos_ref, sin_ref, lg_ref, o_ref, s_out, s_sc, *, n_chunks):
    c = pl.program_id(2)
    chunk = q_ref.shape[0]
    half = RET_DK // 2

    @pl.when(c == 0)
    def _():
        s_sc[...] = jnp.zeros_like(s_sc)

    lg = lg_ref[0]
    cos = cos_ref[...]
    sin = sin_ref[...]

    def rot(x):
        x1, x2 = x[:, :half], x[:, half:]
        return jnp.concatenate([x1 * cos - x2 * sin, x1 * sin + x2 * cos], axis=1)

    qb = rot(q_ref[...]).astype(BF16)
    kr = rot(k_ref[...]) * (RET_DK ** -0.5)
    vb = v_ref[...].astype(BF16)
    l_idx = lax.broadcasted_iota(jnp.int32, (chunk, chunk), 0)
    m_idx = lax.broadcasted_iota(jnp.int32, (chunk, chunk), 1)
    diff = (l_idx - m_idx).astype(F32)
    decay = jnp.where(diff >= 0, jnp.exp(lg * jnp.maximum(diff, 0.0)), 0.0)
    scores = lax.dot_general(qb, kr.astype(BF16), (((1,), (1,)), ((), ())), preferred_element_type=F32) * decay
    intra = jnp.dot(scores.astype(BF16), vb, preferred_element_type=F32)
    state = s_sc[...]
    pos = lax.broadcasted_iota(jnp.int32, (chunk, 1), 0).astype(F32)
    inter = jnp.dot(qb, state.astype(BF16), preferred_element_type=F32) * jnp.exp(lg * (pos + 1.0))
    kd = kr * jnp.exp(lg * (chunk - 1.0 - pos))
    new_state = jnp.exp(lg * chunk) * state + jnp.dot(kd.T.astype(BF16), vb, preferred_element_type=F32)
    s_sc[...] = new_state
    o_ref[...] = _group_norm_gate(intra + inter, g_ref[...]).astype(o_ref.dtype)

    @pl.when(c == n_chunks - 1)
    def _():
        s_out[0, 0] = new_state


def _ret_prompt(proj, *, batch, length, heads):
    chunk = CHUNK if length % CHUNK == 0 else length
    nc = length // chunk
    cos, sin = _rope_tables(jnp.arange(length, dtype=jnp.int32))
    row = lambda b, h, c: b * nc + c
    return pl.pallas_call(
        functools.partial(_ret_prompt_kernel, n_chunks=nc),
        grid=(batch, heads, nc),
        in_specs=[pl.BlockSpec((chunk, RET_DK), lambda b, h, c: (row(b, h, c), h)),
                  pl.BlockSpec((chunk, RET_DK), lambda b, h, c: (row(b, h, c), heads + h)),
                  pl.BlockSpec((chunk, RET_DV), lambda b, h, c: (row(b, h, c), heads + h)),
                  pl.BlockSpec((chunk, RET_DV), lambda b, h, c: (row(b, h, c), 2 * heads + h)),
                  pl.BlockSpec((chunk, RET_DK // 2), lambda b, h, c: (c, 0)),
                  pl.BlockSpec((chunk, RET_DK // 2), lambda b, h, c: (c, 0)),
                  pl.BlockSpec((1, 1, 1), lambda b, h, c: (h, 0, 0))],
        out_specs=[pl.BlockSpec((chunk, RET_DV), lambda b, h, c: (row(b, h, c), h)),
                   pl.BlockSpec((1, 1, RET_DK, RET_DV), lambda b, h, c: (b, h, 0, 0))],
        out_shape=[jax.ShapeDtypeStruct((batch * length, heads * RET_DV), BF16),
                   jax.ShapeDtypeStruct((batch, heads, RET_DK, RET_DV), F32)],
        scratch_shapes=[pltpu.VMEM((RET_DK, RET_DV), F32)],
        compiler_params=_cparams(("parallel", "parallel", "arbitrary"), 32),
        name="ret_prompt",
    )(proj, proj, proj, proj, cos, sin, _ret_log_decay(heads).reshape(heads, 1, 1))


def _ret_sample_kernel(q_ref, k_ref, v_ref, g_ref, cos_ref, sin_ref, lg_ref, s_ref, o_ref, s_out):
    half = RET_DK // 2
    lg = lg_ref[0]
    cos = cos_ref[...]
    sin = sin_ref[...]

    def rot(x):
        x1, x2 = x[:half], x[half:]
        return jnp.concatenate([x1 * cos - x2 * sin, x1 * sin + x2 * cos], axis=0)

    q = rot(q_ref[0, 0])
    k = rot(k_ref[0, 0]) * (RET_DK ** -0.5)
    v = v_ref[0, 0]
    state = s_ref[0, 0, 0]
    gamma = jnp.exp(lg)
    intra = jnp.sum(q * k, axis=0, keepdims=True) * v
    inter = jnp.sum(q * state, axis=0, keepdims=True) * gamma
    s_out[0, 0] = gamma * state + k * v
    o_ref[0, 0] = _group_norm_gate(intra + inter, g_ref[0, 0])


def _ret_sample(proj, state_ret, layer, *, position):
    nb = proj.shape[0]
    heads = state_ret.shape[2]
    qk = heads * RET_DK
    vd = heads * RET_DV
    q = proj[:, :qk].reshape(nb, heads, RET_DK, 1)
    k = proj[:, qk:2 * qk].reshape(nb, heads, RET_DK, 1)
    v = proj[:, 2 * qk:2 * qk + vd].reshape(nb, heads, 1, RET_DV)
    g = proj[:, 2 * qk + vd:].reshape(nb, heads, 1, RET_DV)
    cos, sin = _rope_tables(jnp.full((1,), position, jnp.int32))
    half = RET_DK // 2
    col = pl.BlockSpec((1, 1, RET_DK, 1), lambda b, h: (b, h, 0, 0))
    rowv = pl.BlockSpec((1, 1, 1, RET_DV), lambda b, h: (b, h, 0, 0))
    tab = pl.BlockSpec((half, 1), lambda b, h: (0, 0))
    o, s_new = pl.pallas_call(
        _ret_sample_kernel,
        grid=(nb, heads),
        in_specs=[col, col, rowv, rowv, tab, tab,
                  pl.BlockSpec((1, 1, 1), lambda b, h: (h, 0, 0)),
                  pl.BlockSpec((1, 1, 1, RET_DK, RET_DV), lambda b, h: (layer, b, h, 0, 0))],
        out_specs=[rowv, pl.BlockSpec((1, 1, RET_DK, RET_DV), lambda b, h: (b, h, 0, 0))],
        out_shape=[jax.ShapeDtypeStruct((nb, heads, 1, RET_DV), F32),
                   jax.ShapeDtypeStruct((nb, heads, RET_DK, RET_DV), F32)],
        compiler_params=_cparams(("parallel", "parallel"), 32),
        name="ret_sample",
    )(q, k, v, g, cos.reshape(half, 1), sin.reshape(half, 1),
      _ret_log_decay(heads).reshape(heads, 1, 1), state_ret)
    return o.reshape(nb, vd), s_new


def _softcap(x):
    return GATE_SOFTCAP * jnp.tanh(x / GATE_SOFTCAP)


def _rms_gate(h, gnorm, og):
    hn = h * lax.rsqrt(jnp.mean(h * h, axis=-1, keepdims=True) + LN_EPS) * gnorm
    return jax.nn.sigmoid(og) * hn


def _ml_prompt_kernel(q_ref, k_ref, v_ref, og_ref, gt_ref, gb_ref, gn_ref, h_ref, c_out, n_out, m_out,
                      c_sc, n_sc, m_sc, *, n_chunks):
    c = pl.program_id(2)
    chunk = q_ref.shape[0]

    @pl.when(c == 0)
    def _():
        c_sc[...] = jnp.zeros_like(c_sc)
        n_sc[...] = jnp.zeros_like(n_sc)
        m_sc[...] = jnp.zeros_like(m_sc)

    gates = _softcap(gt_ref[0, 0] + gb_ref[0])
    li_row = gates[0:1, :]
    lf_row = _log_sigmoid(gates[1:2, :])
    t_idx = lax.broadcasted_iota(jnp.int32, (chunk, chunk), 0)
    s_idx = lax.broadcasted_iota(jnp.int32, (chunk, chunk), 1)
    eye = t_idx == s_idx
    causal = s_idx <= t_idx
    lf_col = jnp.sum(jnp.where(eye, lf_row, 0.0), axis=1, keepdims=True)
    li_col = jnp.sum(jnp.where(eye, li_row, 0.0), axis=1, keepdims=True)
    cum_col = jnp.sum(jnp.where(causal, lf_row, 0.0), axis=1, keepdims=True)
    cum_row = jnp.sum(jnp.where(t_idx <= s_idx, lf_col, 0.0), axis=0, keepdims=True)
    m_prev = m_sc[...]
    log_w = jnp.where(causal, cum_col - cum_row + li_row, -jnp.inf)
    log_inter = cum_col + m_prev
    m_t = jnp.maximum(log_inter, jnp.max(log_w, axis=1, keepdims=True))
    w = jnp.exp(log_w - m_t)
    w_inter = jnp.exp(log_inter - m_t)
    q = q_ref[...]
    qb = q.astype(BF16)
    k = k_ref[...] * (ML_DK ** -0.5)
    vb = v_ref[...].astype(BF16)
    s = lax.dot_general(qb, k.astype(BF16), (((1,), (1,)), ((), ())), preferred_element_type=F32) * w
    c_mat = c_sc[...]
    n_vec = n_sc[...]
    num = (jnp.dot(s.astype(BF16), vb, preferred_element_type=F32)
           + jnp.dot(qb, c_mat.astype(BF16), preferred_element_type=F32) * w_inter)
    den = jnp.sum(s, axis=1, keepdims=True) + jnp.sum(q * n_vec, axis=1, keepdims=True) * w_inter
    h = num / jnp.maximum(jnp.abs(den), jnp.exp(-m_t))
    cum_last = cum_col[chunk - 1:chunk, :]
    m_new = m_t[chunk - 1:chunk, :]
    decay = jnp.exp(cum_last + m_prev - m_new)
    kk = k * jnp.exp(cum_last - cum_col + li_col - m_new)
    c_new = decay * c_mat + jnp.dot(kk.T.astype(BF16), vb, preferred_element_type=F32)
    n_new = decay * n_vec + jnp.sum(kk, axis=0, keepdims=True)
    c_sc[...] = c_new
    n_sc[...] = n_new
    m_sc[...] = m_new
    h_ref[...] = _rms_gate(h, gn_ref[0], og_ref[...]).astype(h_ref.dtype)

    @pl.when(c == n_chunks - 1)
    def _():
        c_out[0, 0] = c_new
        n_out[0, 0] = n_new
        m_out[0, 0] = m_new


def _ml_prompt(proj, gates, b_gates, g_norm, *, batch, length, heads):
    chunk = CHUNK if length % CHUNK == 0 else length
    nc = length // chunk
    gt = gates.reshape(batch, length, 2, heads).transpose(0, 3, 2, 1)
    gb = b_gates.astype(F32).reshape(2, heads).T.reshape(heads, 2, 1)
    row = lambda b, h, c: b * nc + c
    hg, c_new, n_new, m_new = pl.pallas_call(
        functools.partial(_ml_prompt_kernel, n_chunks=nc),
        grid=(batch, heads, nc),
        in_specs=[pl.BlockSpec((chunk, ML_DK), lambda b, h, c: (row(b, h, c), h)),
                  pl.BlockSpec((chunk, ML_DK), lambda b, h, c: (row(b, h, c), heads + h)),
                  pl.BlockSpec((chunk, ML_DV), lambda b, h, c: (row(b, h, c), heads + h)),
                  pl.BlockSpec((chunk, ML_DV), lambda b, h, c: (row(b, h, c), 2 * heads + h)),
                  pl.BlockSpec((1, 1, 2, chunk), lambda b, h, c: (b, h, 0, c)),
                  pl.BlockSpec((1, 2, 1), lambda b, h, c: (h, 0, 0)),
                  pl.BlockSpec((1, 1, ML_DV), lambda b, h, c: (h, 0, 0))],
        out_specs=[pl.BlockSpec((chunk, ML_DV), lambda b, h, c: (row(b, h, c), h)),
                   pl.BlockSpec((1, 1, ML_DK, ML_DV), lambda b, h, c: (b, h, 0, 0)),
                   pl.BlockSpec((1, 1, 1, ML_DK), lambda b, h, c: (b, h, 0, 0)),
                   pl.BlockSpec((1, 1, 1, 1), lambda b, h, c: (b, h, 0, 0))],
        out_shape=[jax.ShapeDtypeStruct((batch * length, heads * ML_DV), BF16),
                   jax.ShapeDtypeStruct((batch, heads, ML_DK, ML_DV), F32),
                   jax.ShapeDtypeStruct((batch, heads, 1, ML_DK), F32),
                   jax.ShapeDtypeStruct((batch, heads, 1, 1), F32)],
        scratch_shapes=[pltpu.VMEM((ML_DK, ML_DV), F32), pltpu.VMEM((1, ML_DK), F32), pltpu.VMEM((1, 1), F32)],
        compiler_params=_cparams(("parallel", "parallel", "arbitrary"), 32),
        name="ml_prompt",
    )(proj, proj, proj, proj, gt, gb, g_norm.astype(F32).reshape(heads, 1, ML_DV))
    return hg, c_new, n_new.reshape(batch, heads, ML_DK), m_new.reshape(batch, heads)


def _ml_sample_kernel(q_ref, k_ref, v_ref, og_ref, gt_ref, gb_ref, gn_ref, c_ref, n_ref, m_ref,
                      h_ref, c_out, n_out, m_out):
    q = q_ref[0, 0]
    k = k_ref[0, 0] * (ML_DK ** -0.5)
    v = v_ref[0, 0]
    gates = _softcap(gt_ref[0, 0] + gb_ref[0])
    log_i = gates[0:1, :]
    log_f = _log_sigmoid(gates[1:2, :])
    c_mat = c_ref[0, 0, 0]
    n_vec = n_ref[0, 0, 0]
    m_prev = m_ref[0, 0, 0]
    log_inter = log_f + m_prev
    m_t = jnp.maximum(log_inter, log_i)
    w = jnp.exp(log_i - m_t)
    w_inter = jnp.exp(log_inter - m_t)
    s = jnp.sum(q * k, axis=0, keepdims=True) * w
    num = s * v + jnp.sum(q * c_mat, axis=0, keepdims=True) * w_inter
    den = s + jnp.sum(q * n_vec, axis=0, keepdims=True) * w_inter
    h = num / jnp.maximum(jnp.abs(den), jnp.exp(-m_t))
    decay = jnp.exp(log_f + m_prev - m_t)
    kk = k * jnp.exp(log_i - m_t)
    c_out[0, 0] = decay * c_mat + kk * v
    n_out[0, 0] = decay * n_vec + kk
    m_out[0, 0] = m_t
    h_ref[0, 0] = _rms_gate(h, gn_ref[0], og_ref[0, 0])


def _ml_sample(proj, gates, b_gates, g_norm, state_c, state_n, state_m, layer):
    nb = proj.shape[0]
    heads = state_c.shape[2]
    qk = heads * ML_DK
    vd = heads * ML_DV
    q = proj[:, :qk].reshape(nb, heads, ML_DK, 1)
    k = proj[:, qk:2 * qk].reshape(nb, heads, ML_DK, 1)
    v = proj[:, 2 * qk:2 * qk + vd].reshape(nb, heads, 1, ML_DV)
    og = proj[:, 2 * qk + vd:].reshape(nb, heads, 1, ML_DV)
    gt = gates.reshape(nb, 2, heads).transpose(0, 2, 1).reshape(nb, heads, 2, 1)
    gb = b_gates.astype(F32).reshape(2, heads).T.reshape(heads, 2, 1)
    n_layers = state_c.shape[0]
    col = pl.BlockSpec((1, 1, ML_DK, 1), lambda b, h: (b, h, 0, 0))
    rowv = pl.BlockSpec((1, 1, 1, ML_DV), lambda b, h: (b, h, 0, 0))
    mat = pl.BlockSpec((1, 1, ML_DK, ML_DV), lambda b, h: (b, h, 0, 0))
    one = pl.BlockSpec((1, 1, 1, 1), lambda b, h: (b, h, 0, 0))
    hg, c_new, n_new, m_new = pl.pallas_call(
        _ml_sample_kernel,
        grid=(nb, heads),
        in_specs=[col, col, rowv, rowv,
                  pl.BlockSpec((1, 1, 2, 1), lambda b, h: (b, h, 0, 0)),
                  pl.BlockSpec((1, 2, 1), lambda b, h: (h, 0, 0)),
                  pl.BlockSpec((1, 1, ML_DV), lambda b, h: (h, 0, 0)),
                  pl.BlockSpec((1, 1, 1, ML_DK, ML_DV), lambda b, h: (layer, b, h, 0, 0)),
                  pl.BlockSpec((1, 1, 1, ML_DK, 1), lambda b, h: (layer, b, h, 0, 0)),
                  pl.BlockSpec((1, 1, 1, 1, 1), lambda b, h: (layer, b, h, 0, 0))],
        out_specs=[rowv, mat, col, one],
        out_shape=[jax.ShapeDtypeStruct((nb, heads, 1, ML_DV), F32),
                   jax.ShapeDtypeStruct((nb, heads, ML_DK, ML_DV), F32),
                   jax.ShapeDtypeStruct((nb, heads, ML_DK, 1), F32),
                   jax.ShapeDtypeStruct((nb, heads, 1, 1), F32)],
        compiler_params=_cparams(("parallel", "parallel"), 32),
        name="ml_sample",
    )(q, k, v, og, gt, gb, g_norm.astype(F32).reshape(heads, 1, ML_DV), state_c,
      state_n.reshape(n_layers, nb, heads, ML_DK, 1), state_m.reshape(n_layers, nb, heads, 1, 1))
    return hg.reshape(nb, vd), c_new, n_new.reshape(nb, heads, ML_DK), m_new.reshape(nb, heads)


def kernel(x_prompt, x_sample, cache_k, cache_v, page_table, state_ret, state_mlstm_c, state_mlstm_n,
           state_mlstm_m, ln_g, ln_b, w_ffn_in, w_ffn_out, w_sb_in, w_sb_out, b_sb, w_ret_in, w_ret_out,
           w_ml_in, b_ml_gates, g_ml_norm, w_ml_out):
    batch, length, d = x_prompt.shape
    nb, dec_len, _ = x_sample.shape
    assert dec_len == 1, "the sample group is decoded one token per sequence"
    depth = ln_g.shape[0]
    alpha = (2.0 * depth) ** 0.25
    past_len = page_table.shape[1] * cache_k.shape[2]
    sb_heads = d // SB_HEAD_DIM
    ret_heads = d // RET_DK
    ml_heads = d // ML_DV

    groups = [x_prompt.reshape(batch * length, d), x_sample.reshape(nb, d)]
    groups = [(x, x.astype(BF16)) for x in groups]

    def ffn_half(groups, i, half, ln_idx):
        w_in = w_ffn_in[i, half].astype(BF16)
        w_out = w_ffn_out[i, half].astype(BF16)
        out = []
        for x, xb in groups:
            sub = _matmul(_ffn_in(xb, w_in), w_out)
            out.append(_resid_ln(x, sub, ln_g[i, ln_idx], ln_b[i, ln_idx], alpha=alpha, scale=0.5))
        return out

    k_p, v_p, k_s, v_s, ret_p, ret_s = [], [], [], [], [], []
    mc_p, mn_p, mm_p, mc_s, mn_s, mm_s = [], [], [], [], [], []
    for i in range(depth):
        j, mixer = divmod(i, N_MIXERS)
        groups = ffn_half(groups, i, 0, 0)
        (xp, xpb), (xs, xsb) = groups
        if mixer == 0:
            w_in = w_sb_in[j].astype(BF16)
            w_out = w_sb_out[j].astype(BF16)
            qkv_p = _matmul(xpb, w_in)
            qkv_s = _matmul(xsb, w_in)
            o_p = _sb_prompt(qkv_p, b_sb[j], batch=batch, length=length, heads=sb_heads)
            o_s = _sb_sample(qkv_s[:, :d], cache_k, cache_v, page_table, j, b_sb[j]).astype(BF16)
            kv_shape = (sb_heads, SB_HEAD_DIM)
            k_p.append(qkv_p[:, d:2 * d].reshape(batch, length, *kv_shape))
            v_p.append(qkv_p[:, 2 * d:].reshape(batch, length, *kv_shape))
            k_s.append(qkv_s[:, d:2 * d].reshape(nb, 1, *kv_shape))
            v_s.append(qkv_s[:, 2 * d:].reshape(nb, 1, *kv_shape))
        elif mixer == 1:
            w_in = w_ret_in[j].astype(BF16)
            w_out = w_ret_out[j].astype(BF16)
            o_p, sp = _ret_prompt(_matmul(xpb, w_in), batch=batch, length=length, heads=ret_heads)
            o_s, ss = _ret_sample(_matmul(xsb, w_in), state_ret, j, position=past_len)
            o_s = o_s.astype(BF16)
            ret_p.append(sp)
            ret_s.append(ss)
        else:
            n_main = 2 * ml_heads * ML_DK + 2 * ml_heads * ML_DV
            w_in = w_ml_in[j, :, :n_main].astype(BF16)
            w_gate = jnp.pad(w_ml_in[j, :, n_main:], ((0, 0), (0, LANES - 2 * ml_heads))).astype(BF16)
            w_out = w_ml_out[j].astype(BF16)
            gates_p = _matmul(xpb, w_gate, tn=LANES)[:, :2 * ml_heads]
            gates_s = _matmul(xsb, w_gate, tn=LANES)[:, :2 * ml_heads]
            o_p, cp, np_, mp = _ml_prompt(_matmul(xpb, w_in), gates_p, b_ml_gates[j], g_ml_norm[j],
                                          batch=batch, length=length, heads=ml_heads)
            o_s, cs, ns, ms = _ml_sample(_matmul(xsb, w_in), gates_s, b_ml_gates[j], g_ml_norm[j],
                                         state_mlstm_c, state_mlstm_n, state_mlstm_m, j)
            o_s = o_s.astype(BF16)
            mc_p.append(cp)
            mn_p.append(np_)
            mm_p.append(mp)
            mc_s.append(cs)
            mn_s.append(ns)
            mm_s.append(ms)
        y_p = _matmul(o_p, w_out)
        y_s = _matmul(o_s, w_out)
        groups = [_resid_ln(xp, y_p, ln_g[i, 1], ln_b[i, 1], alpha=alpha, scale=1.0),
                  _resid_ln(xs, y_s, ln_g[i, 1], ln_b[i, 1], alpha=alpha, scale=1.0)]
        groups = ffn_half(groups, i, 1, 2)

    (xp, _), (xs, _) = groups
    return (xp.reshape(batch, length, d), xs.reshape(nb, 1, d),
            jnp.stack(k_p), jnp.stack(v_p), jnp.stack(k_s), jnp.stack(v_s),
            jnp.stack(ret_p), jnp.stack(ret_s),
            jnp.stack(mc_p), jnp.stack(mn_p), jnp.stack(mm_p),
            jnp.stack(mc_s), jnp.stack(mn_s), jnp.stack(mm_s))
```

```python
import functools

import jax
import jax.numpy as jnp
from jax import lax
from jax.experimental import pallas as pl
from jax.experimental.pallas import tpu as pltpu

F32 = jnp.float32
BF16 = jnp.bfloat16

LN_EPS = 1e-5
N_MIXERS = 3
SB_HEAD_DIM = 128
RET_DK = 256
RET_DV = 512
ML_DK = 256
ML_DV = 512
CHUNK = 128
SB_QUERY_TILE = 1024
SB_KEY_TILE = 256
ROPE_BASE = 10000.0
GATE_SOFTCAP = 15.0
LANES = 128
MIB = 1024 * 1024
LOG2E = 1.4426950408889634
MATMUL_PANEL_BUDGET = 36 * MIB
MATMUL_VMEM_BUDGET = 46 * MIB
FFN_ROW_TILE = 2048
FFN_PANEL_BUDGET = 24 * MIB


def _cparams(semantics, vmem_mib):
    return pltpu.CompilerParams(dimension_semantics=semantics, vmem_limit_bytes=vmem_mib * MIB)


def _tile(dim, pref):
    t = min(dim, pref)
    while dim % t:
        t //= 2
    return t


def _log_sigmoid(z):
    return jnp.minimum(z, 0.0) - jnp.log1p(jnp.exp(-jnp.abs(z)))


def _split_bf16(x):
    hi = x.astype(BF16)
    lo = (x - hi.astype(F32)).astype(BF16)
    return hi, lo


def _mm_kernel(a_ref, w_ref, o_ref):
    o_ref[...] = jnp.dot(a_ref[...], w_ref[...], preferred_element_type=F32).astype(o_ref.dtype)


def _matmul(a, w, *, tm=1024, tn=512):
    m, kdim = a.shape
    n = w.shape[1]
    tm, tn = _tile(m, tm), _tile(n, tn)
    panel = tm * kdim * a.dtype.itemsize
    single = 2 * panel > MATMUL_PANEL_BUDGET
    panel *= 1 if single else 2

    def vmem_bytes(tn):
        return panel + 2 * kdim * tn * w.dtype.itemsize + 3 * tm * tn * 4

    while vmem_bytes(tn) > MATMUL_VMEM_BUDGET and tn > LANES:
        tn //= 2
    a_spec = pl.BlockSpec((tm, kdim), lambda i, j: (i, 0),
                          **({"pipeline_mode": pl.Buffered(1)} if single else {}))
    return pl.pallas_call(
        _mm_kernel,
        grid=(m // tm, n // tn),
        in_specs=[a_spec, pl.BlockSpec((kdim, tn), lambda i, j: (0, j))],
        out_specs=pl.BlockSpec((tm, tn), lambda i, j: (i, j)),
        out_shape=jax.ShapeDtypeStruct((m, n), F32),
        compiler_params=_cparams(("parallel", "arbitrary"), vmem_bytes(tn) // MIB + 6),
        name="matmul",
    )(a, w)


def _ffn_in_kernel(x_ref, wg_ref, wu_ref, h_ref, *wb_refs):
    wg = wg_ref[...].astype(BF16)
    wu = wu_ref[...].astype(BF16)
    x = x_ref[...]
    g = jnp.dot(x, wg, preferred_element_type=F32)
    u = jnp.dot(x, wu, preferred_element_type=F32)
    h_ref[...] = (g * jax.nn.sigmoid(g) * u).astype(h_ref.dtype)
    if wb_refs:
        wb_refs[0][...] = wg
        wb_refs[1][...] = wu


def _ffn_in(x_b, w_gate, w_up, f, *, lead=(), up_col=0, emit_bf16_weights=False):
    m, d = x_b.shape
    tm, tn = _tile(m, FFN_ROW_TILE), _tile(f, 256)
    nj = f // tn
    single = 2 * tm * d * x_b.dtype.itemsize > FFN_PANEL_BUDGET
    x_spec = pl.BlockSpec((tm, d), lambda i, j: (i, 0),
                          **({"pipeline_mode": pl.Buffered(1)} if single else {}))

    def w_spec(col0):
        return pl.BlockSpec((None,) * len(lead) + (d, tn), lambda i, j: (*lead, 0, j + col0 // tn))

    out_specs = [pl.BlockSpec((tm, tn), lambda i, j: (i, j))]
    out_shape = [jax.ShapeDtypeStruct((m, f), BF16)]
    if emit_bf16_weights:
        out_specs += [pl.BlockSpec((d, tn), lambda i, j: (0, j))] * 2
        out_shape += [jax.ShapeDtypeStruct((d, f), BF16)] * 2
    return pl.pallas_call(
        _ffn_in_kernel,
        grid=(m // tm, nj),
        in_specs=[x_spec, w_spec(0), w_spec(up_col)],
        out_specs=out_specs,
        out_shape=out_shape,
        compiler_params=_cparams(("parallel", "arbitrary"), 52),
        name="ffn_in",
    )(x_b, w_gate, w_up)


def _resid_ln_kernel(x_ref, s_ref, g_ref, b_ref, y_ref, yb_ref, *, alpha, scale):
    v = alpha * x_ref[...] + scale * s_ref[...]
    mu = jnp.mean(v, axis=-1, keepdims=True)
    d = v - mu
    var = jnp.mean(d * d, axis=-1, keepdims=True)
    y = d * lax.rsqrt(var + LN_EPS) * g_ref[...] + b_ref[...]
    y_ref[...] = y
    yb_ref[...] = y.astype(BF16)


def _resid_ln(x, sub, g, b, *, alpha, scale):
    m, d = x.shape
    tm = _tile(m, 256)
    row = pl.BlockSpec((tm, d), lambda i: (i, 0))
    vec = pl.BlockSpec((1, d), lambda i: (0, 0))
    return pl.pallas_call(
        functools.partial(_resid_ln_kernel, alpha=alpha, scale=scale),
        grid=(m // tm,),
        in_specs=[row, row, vec, vec],
        out_specs=[row, row],
        out_shape=[jax.ShapeDtypeStruct((m, d), F32), jax.ShapeDtypeStruct((m, d), BF16)],
        compiler_params=_cparams(("parallel",), 48),
        name="resid_ln",
    )(x, sub, g.reshape(1, d), b.reshape(1, d))


def _sb_prompt_kernel(q_ref, k_ref, v_ref, bias_ref, tri_ref, o_ref, *, scale):
    i = pl.program_id(2)
    tq = q_ref.shape[0]
    kw = tri_ref.shape[0]
    n_diag = tq // kw
    hd = v_ref.shape[1]
    q = q_ref[...].astype(BF16)
    bias2 = bias_ref[0] * LOG2E
    scale2 = scale * LOG2E
    tri = tri_ref[...]
    tri2 = jnp.concatenate([tri, tri], axis=0)

    def sweep(kb, carry, acc, first_row, diagonal):
        rows = tq - first_row
        off = pl.multiple_of(kb * kw, kw)
        kblk = k_ref[pl.ds(off, kw), :].astype(BF16)
        vblk = v_ref[pl.ds(off, kw), :].astype(BF16)
        z = lax.dot_general(q[first_row:], kblk, (((1,), (1,)), ((), ())), preferred_element_type=F32)
        z = z * scale2 + bias2
        l1m = -(jnp.maximum(z, 0.0) + jnp.log(1.0 + jnp.exp2(-jnp.abs(z))) * LOG2E)
        log_beta = z + l1m
        if diagonal:
            visible = (lax.broadcasted_iota(jnp.int32, (rows, kw), 1)
                       < lax.broadcasted_iota(jnp.int32, (rows, kw), 0))
            l1m = jnp.where(visible, l1m, 0.0)
        hi, lo = _split_bf16(l1m)
        stick = jnp.dot(jnp.concatenate([hi, lo], axis=1), tri2, preferred_element_type=F32) + carry[first_row:]
        wgt = jnp.exp2(log_beta + stick)
        if diagonal:
            wgt = jnp.where(visible, wgt, 0.0)
        acc_new = acc[first_row:] + jnp.dot(wgt.astype(BF16), vblk, preferred_element_type=F32)
        carry_new = carry[first_row:] + jnp.sum(l1m, axis=1, keepdims=True)
        if first_row:
            acc_new = jnp.concatenate([acc[:first_row], acc_new], axis=0)
            carry_new = jnp.concatenate([carry[:first_row], carry_new], axis=0)
        return carry_new, acc_new

    carry = jnp.zeros((tq, 1), F32)
    acc = jnp.zeros((tq, hd), F32)
    for r in reversed(range(n_diag)):
        carry, acc = sweep(i * n_diag + r, carry, acc, r * kw, True)

    def body(it, state):
        for r in range(n_diag):
            state = sweep((i - it) * n_diag - 1 - r, state[0], state[1], 0, False)
        return state

    carry, acc = lax.fori_loop(0, i, body, (carry, acc))
    o_ref[...] = acc.astype(o_ref.dtype)


def _tri_lower(n):
    j = lax.broadcasted_iota(jnp.int32, (n, n), 0)
    s = lax.broadcasted_iota(jnp.int32, (n, n), 1)
    return (j > s).astype(BF16)


def _sb_prompt(qkv, bias, *, batch, length, heads):
    hd = SB_HEAD_DIM
    tq, kw = _tile(length, SB_QUERY_TILE), _tile(length, SB_KEY_TILE)
    nq = length // tq
    return pl.pallas_call(
        functools.partial(_sb_prompt_kernel, scale=hd ** -0.5),
        grid=(batch, heads, nq),
        in_specs=[pl.BlockSpec((tq, hd), lambda b, h, i: (b * nq + i, h)),
                  pl.BlockSpec((length, hd), lambda b, h, i: (b, heads + h)),
                  pl.BlockSpec((length, hd), lambda b, h, i: (b, 2 * heads + h)),
                  pl.BlockSpec((1, 1, 1), lambda b, h, i: (h, 0, 0)),
                  pl.BlockSpec((kw, kw), lambda b, h, i: (0, 0))],
        out_specs=pl.BlockSpec((tq, hd), lambda b, h, i: (b * nq + i, h)),
        out_shape=jax.ShapeDtypeStruct((batch * length, heads * hd), BF16),
        compiler_params=_cparams(("parallel", "parallel", "arbitrary"), 32),
        name="sb_prompt",
    )(qkv, qkv, qkv, bias.astype(F32).reshape(heads, 1, 1), _tri_lower(kw))


def _sb_sample_kernel(pt_ref, q_ref, kp_ref, vp_ref, bias_ref, o_ref, run_sc, acc_sc, *, scale, n_pages):
    del pt_ref
    p = pl.program_id(1)
    page, heads, hd = kp_ref.shape

    @pl.when(p == 0)
    def _():
        run_sc[...] = jnp.zeros_like(run_sc)
        acc_sc[...] = jnp.zeros_like(acc_sc)

    kq = (kp_ref[...] * q_ref[0]).reshape(page * heads, hd)
    hi, lo = _split_bf16(kq)
    ones = jnp.ones((hd, hd), BF16)
    z = jnp.dot(hi, ones, preferred_element_type=F32) + jnp.dot(lo, ones, preferred_element_type=F32)
    z = z.reshape(page, heads, hd) * (scale * LOG2E) + bias_ref[...] * LOG2E
    l1m = -(jnp.maximum(z, 0.0) + jnp.log(1.0 + jnp.exp2(-jnp.abs(z))) * LOG2E)
    log_beta = z + l1m
    run = run_sc[...]
    acc = acc_sc[...]
    for s in reversed(range(page)):
        acc = acc + jnp.exp2(log_beta[s] + run) * vp_ref[s]
        run = run + l1m[s]
    run_sc[...] = run
    acc_sc[...] = acc

    @pl.when(p == n_pages - 1)
    def _():
        o_ref[0] = acc


def _sb_sample(q, cache_k, cache_v, page_table, layer, bias):
    nb, d = q.shape
    page, heads, hd = cache_k.shape[2:]
    n_pages = page_table.shape[1]
    page_spec = pl.BlockSpec((None, None, page, heads, hd),
                             lambda b, p, pt: (layer, pt[b, n_pages - 1 - p], 0, 0, 0))
    row_spec = pl.BlockSpec((1, heads, hd), lambda b, p, pt: (b, 0, 0))
    out = pl.pallas_call(
        functools.partial(_sb_sample_kernel, scale=hd ** -0.5, n_pages=n_pages),
        grid_spec=pltpu.PrefetchScalarGridSpec(
            num_scalar_prefetch=1,
            grid=(nb, n_pages),
            in_specs=[row_spec, page_spec, page_spec,
                      pl.BlockSpec((heads, 1), lambda b, p, pt: (0, 0))],
            out_specs=row_spec,
            scratch_shapes=[pltpu.VMEM((heads, hd), F32), pltpu.VMEM((heads, hd), F32)]),
        out_shape=jax.ShapeDtypeStruct((nb, heads, hd), F32),
        compiler_params=_cparams(("parallel", "arbitrary"), 32),
        name="sb_sample",
    )(page_table, q.reshape(nb, heads, hd), cache_k, cache_v, bias.astype(F32).reshape(heads, 1))
    return out.reshape(nb, d)


def _rope_tables(positions):
    half = RET_DK // 2
    inv_freq = ROPE_BASE ** (-jnp.arange(half, dtype=F32) / half)
    ang = positions.astype(F32)[:, None] * inv_freq[None, :]
    return jnp.cos(ang), jnp.sin(ang)


def _ret_log_decay(heads):
    return jnp.log1p(-jnp.exp2(-5.0 - jnp.arange(heads, dtype=F32)))


def _group_norm_gate(o, g):
    mu = jnp.mean(o, axis=-1, keepdims=True)
    d = o - mu
    var = jnp.mean(d * d, axis=-1, keepdims=True)
    return g * jax.nn.sigmoid(g) * (d * lax.rsqrt(var + LN_EPS))


def _ret_prompt_kernel(q_ref, k_ref, v_ref, g_ref, cos_ref, sin_ref, lg_ref, o_ref, s_out, s_sc, *, n_chunks):
    c = pl.program_id(2)
    chunk = q_ref.shape[0]
    half = RET_DK // 2

    @pl.when(c == 0)
    def _():
        s_sc[...] = jnp.zeros_like(s_sc)

    lg = lg_ref[0]
    cos = cos_ref[...]
    sin = sin_ref[...]

    def rot(x):
        x1, x2 = x[:, :half], x[:, half:]
        return jnp.concatenate([x1 * cos - x2 * sin, x1 * sin + x2 * cos], axis=1)

    qb = rot(q_ref[...]).astype(BF16)
    kr = rot(k_ref[...]) * (RET_DK ** -0.5)
    vb = v_ref[...].astype(BF16)
    l_idx = lax.broadcasted_iota(jnp.int32, (chunk, chunk), 0)
    m_idx = lax.broadcasted_iota(jnp.int32, (chunk, chunk), 1)
    diff = (l_idx - m_idx).astype(F32)
    decay = jnp.where(diff >= 0, jnp.exp(lg * jnp.maximum(diff, 0.0)), 0.0)
    scores = lax.dot_general(qb, kr.astype(BF16), (((1,), (1,)), ((), ())), preferred_element_type=F32) * decay
    intra = jnp.dot(scores.astype(BF16), vb, preferred_element_type=F32)
    state = s_sc[...]
    pos = lax.broadcasted_iota(jnp.int32, (chunk, 1), 0).astype(F32)
    inter = jnp.dot(qb, state.astype(BF16), preferred_element_type=F32) * jnp.exp(lg * (pos + 1.0))
    kd = kr * jnp.exp(lg * (chunk - 1.0 - pos))
    new_state = jnp.exp(lg * chunk) * state + jnp.dot(kd.T.astype(BF16), vb, preferred_element_type=F32)
    s_sc[...] = new_state
    o_ref[...] = _group_norm_gate(intra + inter, g_ref[...]).astype(o_ref.dtype)

    @pl.when(c == n_chunks - 1)
    def _():
        s_out[0, 0] = new_state


def _ret_prompt(proj, *, batch, length, heads):
    chunk = CHUNK if length % CHUNK == 0 else length
    nc = length // chunk
    cos, sin = _rope_tables(jnp.arange(length, dtype=jnp.int32))
    row = lambda b, h, c: b * nc + c
    return pl.pallas_call(
        functools.partial(_ret_prompt_kernel, n_chunks=nc),
        grid=(batch, heads, nc),
        in_specs=[pl.BlockSpec((chunk, RET_DK), lambda b, h, c: (row(b, h, c), h)),
                  pl.BlockSpec((chunk, RET_DK), lambda b, h, c: (row(b, h, c), heads + h)),
                  pl.BlockSpec((chunk, RET_DV), lambda b, h, c: (row(b, h, c), heads + h)),
                  pl.BlockSpec((chunk, RET_DV), lambda b, h, c: (row(b, h, c), 2 * heads + h)),
                  pl.BlockSpec((chunk, RET_DK // 2), lambda b, h, c: (c, 0)),
                  pl.BlockSpec((chunk, RET_DK // 2), lambda b, h, c: (c, 0)),
                  pl.BlockSpec((1, 1, 1), lambda b, h, c: (h, 0, 0))],
        out_specs=[pl.BlockSpec((chunk, RET_DV), lambda b, h, c: (row(b, h, c), h)),
                   pl.BlockSpec((1, 1, RET_DK, RET_DV), lambda b, h, c: (b, h, 0, 0))],
        out_shape=[jax.ShapeDtypeStruct((batch * length, heads * RET_DV), BF16),
                   jax.ShapeDtypeStruct((batch, heads, RET_DK, RET_DV), F32)],
        scratch_shapes=[pltpu.VMEM((RET_DK, RET_DV), F32)],
        compiler_params=_cparams(("parallel", "parallel", "arbitrary"), 32),
        name="ret_prompt",
    )(proj, proj, proj, proj, cos, sin, _ret_log_decay(heads).reshape(heads, 1, 1))


def _ret_sample_kernel(q_ref, k_ref, v_ref, g_ref, cos_ref, sin_ref, lg_ref, s_ref, o_ref, s_out):
    half = RET_DK // 2
    lg = lg_ref[0]
    cos = cos_ref[...]
    sin = sin_ref[...]

    def rot(x):
        x1, x2 = x[:half], x[half:]
        return jnp.concatenate([x1 * cos - x2 * sin, x1 * sin + x2 * cos], axis=0)

    q = rot(q_ref[0, 0])
    k = rot(k_ref[0, 0]) * (RET_DK ** -0.5)
    v = v_ref[0, 0]
    state = s_ref[0, 0, 0]
    gamma = jnp.exp(lg)
    intra = jnp.sum(q * k, axis=0, keepdims=True) * v
    inter = jnp.sum(q * state, axis=0, keepdims=True) * gamma
    s_out[0, 0] = gamma * state + k * v
    o_ref[0, 0] = _group_norm_gate(intra + inter, g_ref[0, 0])


def _ret_sample(proj, state_ret, layer, *, position):
    nb = proj.shape[0]
    heads = state_ret.shape[2]
    qk = heads * RET_DK
    vd = heads * RET_DV
    q = proj[:, :qk].reshape(nb, heads, RET_DK, 1)
    k = proj[:, qk:2 * qk].reshape(nb, heads, RET_DK, 1)
    v = proj[:, 2 * qk:2 * qk + vd].reshape(nb, heads, 1, RET_DV)
    g = proj[:, 2 * qk + vd:].reshape(nb, heads, 1, RET_DV)
    cos, sin = _rope_tables(jnp.full((1,), position, jnp.int32))
    half = RET_DK // 2
    col = pl.BlockSpec((1, 1, RET_DK, 1), lambda b, h: (b, h, 0, 0))
    rowv = pl.BlockSpec((1, 1, 1, RET_DV), lambda b, h: (b, h, 0, 0))
    tab = pl.BlockSpec((half, 1), lambda b, h: (0, 0))
    o, s_new = pl.pallas_call(
        _ret_sample_kernel,
        grid=(nb, heads),
        in_specs=[col, col, rowv, rowv, tab, tab,
                  pl.BlockSpec((1, 1, 1), lambda b, h: (h, 0, 0)),
                  pl.BlockSpec((1, 1, 1, RET_DK, RET_DV), lambda b, h: (layer, b, h, 0, 0))],
        out_specs=[rowv, pl.BlockSpec((1, 1, RET_DK, RET_DV), lambda b, h: (b, h, 0, 0))],
        out_shape=[jax.ShapeDtypeStruct((nb, heads, 1, RET_DV), F32),
                   jax.ShapeDtypeStruct((nb, heads, RET_DK, RET_DV), F32)],
        compiler_params=_cparams(("parallel", "parallel"), 32),
        name="ret_sample",
    )(q, k, v, g, cos.reshape(half, 1), sin.reshape(half, 1),
      _ret_log_decay(heads).reshape(heads, 1, 1), state_ret)
    return o.reshape(nb, vd), s_new


def _softcap(x):
    return GATE_SOFTCAP * jnp.tanh(x / GATE_SOFTCAP)


def _rms_gate(h, gnorm, og):
    hn = h * lax.rsqrt(jnp.mean(h * h, axis=-1, keepdims=True) + LN_EPS) * gnorm
    return jax.nn.sigmoid(og) * hn


def _ml_prompt_kernel(q_ref, k_ref, v_ref, og_ref, gt_ref, gb_ref, gn_ref, h_ref, c_out, n_out, m_out,
                      c_sc, n_sc, m_sc, *, n_chunks):
    c = pl.program_id(2)
    chunk = q_ref.shape[0]

    @pl.when(c == 0)
    def _():
        c_sc[...] = jnp.zeros_like(c_sc)
        n_sc[...] = jnp.zeros_like(n_sc)
        m_sc[...] = jnp.zeros_like(m_sc)

    gates = _softcap(gt_ref[0, 0] + gb_ref[0])
    li_row = gates[0:1, :]
    lf_row = _log_sigmoid(gates[1:2, :])
    t_idx = lax.broadcasted_iota(jnp.int32, (chunk, chunk), 0)
    s_idx = lax.broadcasted_iota(jnp.int32, (chunk, chunk), 1)
    eye = t_idx == s_idx
    causal = s_idx <= t_idx
    lf_col = jnp.sum(jnp.where(eye, lf_row, 0.0), axis=1, keepdims=True)
    li_col = jnp.sum(jnp.where(eye, li_row, 0.0), axis=1, keepdims=True)
    cum_col = jnp.sum(jnp.where(causal, lf_row, 0.0), axis=1, keepdims=True)
    cum_row = jnp.sum(jnp.where(t_idx <= s_idx, lf_col, 0.0), axis=0, keepdims=True)
    m_prev = m_sc[...]
    log_w = jnp.where(causal, cum_col - cum_row + li_row, -jnp.inf)
    log_inter = cum_col + m_prev
    m_t = jnp.maximum(log_inter, jnp.max(log_w, axis=1, keepdims=True))
    w = jnp.exp(log_w - m_t)
    w_inter = jnp.exp(log_inter - m_t)
    q = q_ref[...]
    qb = q.astype(BF16)
    k = k_ref[...] * (ML_DK ** -0.5)
    vb = v_ref[...].astype(BF16)
    s = lax.dot_general(qb, k.astype(BF16), (((1,), (1,)), ((), ())), preferred_element_type=F32) * w
    c_mat = c_sc[...]
    n_vec = n_sc[...]
    num = (jnp.dot(s.astype(BF16), vb, preferred_element_type=F32)
           + jnp.dot(qb, c_mat.astype(BF16), preferred_element_type=F32) * w_inter)
    den = jnp.sum(s, axis=1, keepdims=True) + jnp.sum(q * n_vec, axis=1, keepdims=True) * w_inter
    h = num / jnp.maximum(jnp.abs(den), jnp.exp(-m_t))
    cum_last = cum_col[chunk - 1:chunk, :]
    m_new = m_t[chunk - 1:chunk, :]
    decay = jnp.exp(cum_last + m_prev - m_new)
    kk = k * jnp.exp(cum_last - cum_col + li_col - m_new)
    c_new = decay * c_mat + jnp.dot(kk.T.astype(BF16), vb, preferred_element_type=F32)
    n_new = decay * n_vec + jnp.sum(kk, axis=0, keepdims=True)
    c_sc[...] = c_new
    n_sc[...] = n_new
    m_sc[...] = m_new
    h_ref[...] = _rms_gate(h, gn_ref[0], og_ref[...]).astype(h_ref.dtype)

    @pl.when(c == n_chunks - 1)
    def _():
        c_out[0, 0] = c_new
        n_out[0, 0] = n_new
        m_out[0, 0] = m_new


def _ml_prompt(proj, gates, b_gates, g_norm, *, batch, length, heads):
    chunk = CHUNK if length % CHUNK == 0 else length
    nc = length // chunk
    gt = gates.reshape(batch, length, 2, heads).transpose(0, 3, 2, 1)
    gb = b_gates.astype(F32).reshape(2, heads).T.reshape(heads, 2, 1)
    row = lambda b, h, c: b * nc + c
    hg, c_new, n_new, m_new = pl.pallas_call(
        functools.partial(_ml_prompt_kernel, n_chunks=nc),
        grid=(batch, heads, nc),
        in_specs=[pl.BlockSpec((chunk, ML_DK), lambda b, h, c: (row(b, h, c), h)),
                  pl.BlockSpec((chunk, ML_DK), lambda b, h, c: (row(b, h, c), heads + h)),
                  pl.BlockSpec((chunk, ML_DV), lambda b, h, c: (row(b, h, c), heads + h)),
                  pl.BlockSpec((chunk, ML_DV), lambda b, h, c: (row(b, h, c), 2 * heads + h)),
                  pl.BlockSpec((1, 1, 2, chunk), lambda b, h, c: (b, h, 0, c)),
                  pl.BlockSpec((1, 2, 1), lambda b, h, c: (h, 0, 0)),
                  pl.BlockSpec((1, 1, ML_DV), lambda b, h, c: (h, 0, 0))],
        out_specs=[pl.BlockSpec((chunk, ML_DV), lambda b, h, c: (row(b, h, c), h)),
                   pl.BlockSpec((1, 1, ML_DK, ML_DV), lambda b, h, c: (b, h, 0, 0)),
                   pl.BlockSpec((1, 1, 1, ML_DK), lambda b, h, c: (b, h, 0, 0)),
                   pl.BlockSpec((1, 1, 1, 1), lambda b, h, c: (b, h, 0, 0))],
        out_shape=[jax.ShapeDtypeStruct((batch * length, heads * ML_DV), BF16),
                   jax.ShapeDtypeStruct((batch, heads, ML_DK, ML_DV), F32),
                   jax.ShapeDtypeStruct((batch, heads, 1, ML_DK), F32),
                   jax.ShapeDtypeStruct((batch, heads, 1, 1), F32)],
        scratch_shapes=[pltpu.VMEM((ML_DK, ML_DV), F32), pltpu.VMEM((1, ML_DK), F32), pltpu.VMEM((1, 1), F32)],
        compiler_params=_cparams(("parallel", "parallel", "arbitrary"), 32),
        name="ml_prompt",
    )(proj, proj, proj, proj, gt, gb, g_norm.astype(F32).reshape(heads, 1, ML_DV))
    return hg, c_new, n_new.reshape(batch, heads, ML_DK), m_new.reshape(batch, heads)


def _ml_sample_kernel(q_ref, k_ref, v_ref, og_ref, gt_ref, gb_ref, gn_ref, c_ref, n_ref, m_ref,
                      h_ref, c_out, n_out, m_out):
    q = q_ref[0, 0]
    k = k_ref[0, 0] * (ML_DK ** -0.5)
    v = v_ref[0, 0]
    gates = _softcap(gt_ref[0, 0] + gb_ref[0])
    log_i = gates[0:1, :]
    log_f = _log_sigmoid(gates[1:2, :])
    c_mat = c_ref[0, 0, 0]
    n_vec = n_ref[0, 0, 0]
    m_prev = m_ref[0, 0, 0]
    log_inter = log_f + m_prev
    m_t = jnp.maximum(log_inter, log_i)
    w = jnp.exp(log_i - m_t)
    w_inter = jnp.exp(log_inter - m_t)
    s = jnp.sum(q * k, axis=0, keepdims=True) * w
    num = s * v + jnp.sum(q * c_mat, axis=0, keepdims=True) * w_inter
    den = s + jnp.sum(q * n_vec, axis=0, keepdims=True) * w_inter
    h = num / jnp.maximum(jnp.abs(den), jnp.exp(-m_t))
    decay = jnp.exp(log_f + m_prev - m_t)
    kk = k * jnp.exp(log_i - m_t)
    c_out[0, 0] = decay * c_mat + kk * v
    n_out[0, 0] = decay * n_vec + kk
    m_out[0, 0] = m_t
    h_ref[0, 0] = _rms_gate(h, gn_ref[0], og_ref[0, 0])


def _ml_sample(proj, gates, b_gates, g_norm, state_c, state_n, state_m, layer):
    nb = proj.shape[0]
    heads = state_c.shape[2]
    qk = heads * ML_DK
    vd = heads * ML_DV
    q = proj[:, :qk].reshape(nb, heads, ML_DK, 1)
    k = proj[:, qk:2 * qk].reshape(nb, heads, ML_DK, 1)
    v = proj[:, 2 * qk:2 * qk + vd].reshape(nb, heads, 1, ML_DV)
    og = proj[:, 2 * qk + vd:].reshape(nb, heads, 1, ML_DV)
    gt = gates.reshape(nb, 2, heads).transpose(0, 2, 1).reshape(nb, heads, 2, 1)
    gb = b_gates.astype(F32).reshape(2, heads).T.reshape(heads, 2, 1)
    n_layers = state_c.shape[0]
    col = pl.BlockSpec((1, 1, ML_DK, 1), lambda b, h: (b, h, 0, 0))
    rowv = pl.BlockSpec((1, 1, 1, ML_DV), lambda b, h: (b, h, 0, 0))
    mat = pl.BlockSpec((1, 1, ML_DK, ML_DV), lambda b, h: (b, h, 0, 0))
    one = pl.BlockSpec((1, 1, 1, 1), lambda b, h: (b, h, 0, 0))
    hg, c_new, n_new, m_new = pl.pallas_call(
        _ml_sample_kernel,
        grid=(nb, heads),
        in_specs=[col, col, rowv, rowv,
                  pl.BlockSpec((1, 1, 2, 1), lambda b, h: (b, h, 0, 0)),
                  pl.BlockSpec((1, 2, 1), lambda b, h: (h, 0, 0)),
                  pl.BlockSpec((1, 1, ML_DV), lambda b, h: (h, 0, 0)),
                  pl.BlockSpec((1, 1, 1, ML_DK, ML_DV), lambda b, h: (layer, b, h, 0, 0)),
                  pl.BlockSpec((1, 1, 1, ML_DK, 1), lambda b, h: (layer, b, h, 0, 0)),
                  pl.BlockSpec((1, 1, 1, 1, 1), lambda b, h: (layer, b, h, 0, 0))],
        out_specs=[rowv, mat, col, one],
        out_shape=[jax.ShapeDtypeStruct((nb, heads, 1, ML_DV), F32),
                   jax.ShapeDtypeStruct((nb, heads, ML_DK, ML_DV), F32),
                   jax.ShapeDtypeStruct((nb, heads, ML_DK, 1), F32),
                   jax.ShapeDtypeStruct((nb, heads, 1, 1), F32)],
        compiler_params=_cparams(("parallel", "parallel"), 32),
        name="ml_sample",
    )(q, k, v, og, gt, gb, g_norm.astype(F32).reshape(heads, 1, ML_DV), state_c,
      state_n.reshape(n_layers, nb, heads, ML_DK, 1), state_m.reshape(n_layers, nb, heads, 1, 1))
    return hg.reshape(nb, vd), c_new, n_new.reshape(nb, heads, ML_DK), m_new.reshape(nb, heads)


def kernel(x_prompt, x_sample, cache_k, cache_v, page_table, state_ret, state_mlstm_c, state_mlstm_n,
           state_mlstm_m, ln_g, ln_b, w_ffn_in, w_ffn_out, w_sb_in, w_sb_out, b_sb, w_ret_in, w_ret_out,
           w_ml_in, b_ml_gates, g_ml_norm, w_ml_out):
    batch, length, d = x_prompt.shape
    nb, dec_len, _ = x_sample.shape
    assert dec_len == 1, "the sample group is decoded one token per sequence"
    depth = ln_g.shape[0]
    alpha = (2.0 * depth) ** 0.25
    past_len = page_table.shape[1] * cache_k.shape[2]
    sb_heads = d // SB_HEAD_DIM
    ret_heads = d // RET_DK
    ml_heads = d // ML_DV

    groups = [x_prompt.reshape(batch * length, d), x_sample.reshape(nb, d)]
    groups = [(x, x.astype(BF16)) for x in groups]

    def ffn_half(groups, i, half, ln_idx):
        w_out = w_ffn_out[i, half].astype(BF16)
        (xp, xpb), (xs, xsb) = groups
        f = w_ffn_in.shape[-1] // 2
        h_p, wg_b, wu_b = _ffn_in(xpb, w_ffn_in, w_ffn_in, f, lead=(i, half), up_col=f, emit_bf16_weights=True)
        h_s, = _ffn_in(xsb, wg_b, wu_b, f)
        return [_resid_ln(x, _matmul(h, w_out), ln_g[i, ln_idx], ln_b[i, ln_idx], alpha=alpha, scale=0.5)
                for x, h in ((xp, h_p), (xs, h_s))]

    k_p, v_p, k_s, v_s, ret_p, ret_s = [], [], [], [], [], []
    mc_p, mn_p, mm_p, mc_s, mn_s, mm_s = [], [], [], [], [], []
    for i in range(depth):
        j, mixer = divmod(i, N_MIXERS)
        groups = ffn_half(groups, i, 0, 0)
        (xp, xpb), (xs, xsb) = groups
        if mixer == 0:
            w_in = w_sb_in[j].astype(BF16)
            w_out = w_sb_out[j].astype(BF16)
            qkv_p = _matmul(xpb, w_in)
            qkv_s = _matmul(xsb, w_in)
            o_p = _sb_prompt(qkv_p, b_sb[j], batch=batch, length=length, heads=sb_heads)
            o_s = _sb_sample(qkv_s[:, :d], cache_k, cache_v, page_table, j, b_sb[j]).astype(BF16)
            kv_shape = (sb_heads, SB_HEAD_DIM)
            k_p.append(qkv_p[:, d:2 * d].reshape(batch, length, *kv_shape))
            v_p.append(qkv_p[:, 2 * d:].reshape(batch, length, *kv_shape))
            k_s.append(qkv_s[:, d:2 * d].reshape(nb, 1, *kv_shape))
            v_s.append(qkv_s[:, 2 * d:].reshape(nb, 1, *kv_shape))
        elif mixer == 1:
            w_in = w_ret_in[j].astype(BF16)
            w_out = w_ret_out[j].astype(BF16)
            o_p, sp = _ret_prompt(_matmul(xpb, w_in), batch=batch, length=length, heads=ret_heads)
            o_s, ss = _ret_sample(_matmul(xsb, w_in), state_ret, j, position=past_len)
            o_s = o_s.astype(BF16)
            ret_p.append(sp)
            ret_s.append(ss)
        else:
            n_main = 2 * ml_heads * ML_DK + 2 * ml_heads * ML_DV
            w_in = w_ml_in[j, :, :n_main].astype(BF16)
            w_gate = jnp.pad(w_ml_in[j, :, n_main:], ((0, 0), (0, LANES - 2 * ml_heads))).astype(BF16)
            w_out = w_ml_out[j].astype(BF16)
            gates_p = _matmul(xpb, w_gate, tn=LANES)[:, :2 * ml_heads]
            gates_s = _matmul(xsb, w_gate, tn=LANES)[:, :2 * ml_heads]
            o_p, cp, np_, mp = _ml_prompt(_matmul(xpb, w_in), gates_p, b_ml_gates[j], g_ml_norm[j],
                                          batch=batch, length=length, heads=ml_heads)
            o_s, cs, ns, ms = _ml_sample(_matmul(xsb, w_in), gates_s, b_ml_gates[j], g_ml_norm[j],
                                         state_mlstm_c, state_mlstm_n, state_mlstm_m, j)
            o_s = o_s.astype(BF16)
            mc_p.append(cp)
            mn_p.append(np_)
            mm_p.append(mp)
            mc_s.append(cs)
            mn_s.append(ns)
            mm_s.append(ms)
        y_p = _matmul(o_p, w_out)
        y_s = _matmul(o_s, w_out)
        groups = [_resid_ln(xp, y_p, ln_g[i, 1], ln_b[i, 1], alpha=alpha, scale=1.0),
                  _resid_ln(xs, y_s, ln_g[i, 1], ln_b[i, 1], alpha=alpha, scale=1.0)]
        groups = ffn_half(groups, i, 1, 2)

    (xp, _), (xs, _) = groups
    return (xp.reshape(batch, length, d), xs.reshape(nb, 1, d),
            jnp.stack(k_p), jnp.stack(v_p), jnp.stack(k_s), jnp.stack(v_s),
            jnp.stack(ret_p), jnp.stack(ret_s),
            jnp.stack(mc_p), jnp.stack(mn_p), jnp.stack(mm_p),
            jnp.stack(mc_s), jnp.stack(mn_s), jnp.stack(mm_s))
```
